```python
import math
import jax, jax.numpy as jnp
from jax import lax
import numpy as np

D_MODEL = 1024
BATCH = 4
SEQ = 8192
DEPTH = 2

CHUNK = 64
QBLOCK = 128
PLE_DIM = 256
N_BRANCH = 4
GDN_HEADS = 4
GDN_HEAD_DIM = 128
GDN_WIDTH = GDN_HEADS * GDN_HEAD_DIM
GDN_CONV = 4
CONV_WIDTH = D_MODEL // 2
CONV_K = 31
SB_HEADS = 8
SB_HEAD_DIM = 64
SB_WIDTH = SB_HEADS * SB_HEAD_DIM
FOX_HEADS = 8
FOX_HEAD_DIM = 64
FOX_WIDTH = FOX_HEADS * FOX_HEAD_DIM
BRANCH_WIDTH = 512
ALPHA = (2 * DEPTH) ** 0.25
BETA = (8 * DEPTH) ** -0.25
EPS = 1e-5

SIZES = ([GDN_WIDTH] * 4 + [GDN_HEADS, GDN_HEADS]
         + [2 * CONV_WIDTH, CONV_WIDTH]
         + [SB_WIDTH] * 4
         + [FOX_WIDTH] * 4 + [FOX_HEADS]
         + [N_BRANCH * D_MODEL])
PROJ_WIDTH = sum(SIZES)
SPLIT_POINTS = [sum(SIZES[:i + 1]) for i in range(len(SIZES) - 1)]

kernel_name = "hybrid_gdn_conformer_stickbreak_fox_encoder"


def layer_norm(x, g, b):
    xf = x.astype(jnp.float32)
    mu = jnp.mean(xf, axis=-1, keepdims=True)
    var = jnp.mean(jnp.square(xf - mu), axis=-1, keepdims=True)
    return ((xf - mu) * lax.rsqrt(var + EPS) * g.astype(jnp.float32) + b.astype(jnp.float32)).astype(x.dtype)


def rms_norm(x, g):
    xf = x.astype(jnp.float32)
    return xf * lax.rsqrt(jnp.mean(jnp.square(xf), axis=-1, keepdims=True) + EPS) * g.astype(jnp.float32)


def l2_normalize(x):
    xf = x.astype(jnp.float32)
    return xf * lax.rsqrt(jnp.sum(jnp.square(xf), axis=-1, keepdims=True) + 1e-6)


def causal_dwconv(x, w):
    k_width, c = w.shape
    xp = jnp.pad(x, ((0, 0), (k_width - 1, 0), (0, 0)))
    return lax.conv_general_dilated(xp, w[:, None, :].astype(x.dtype), window_strides=(1,), padding='VALID',
                                    dimension_numbers=('NWC', 'WIO', 'NWC'), feature_group_count=c)


def gated_delta_rule(q, k, v, g, beta):
    B, T, H, dk = q.shape
    dv = v.shape[-1]
    n = T // CHUNK
    f32 = jnp.float32

    def chunks(a):
        a = a.astype(f32).reshape((B, n, CHUNK, H) + a.shape[3:])
        return jnp.moveaxis(a, 3, 1)

    qc = chunks(q) * dk ** -0.5
    kc, vc, bc = chunks(k), chunks(v), chunks(beta)
    gc = jnp.cumsum(chunks(g), axis=-1)
    idx = jnp.arange(CHUNK)
    incl = idx[:, None] >= idx[None, :]
    strict = idx[:, None] > idx[None, :]
    decay = jnp.where(incl, jnp.exp(jnp.where(incl, gc[..., :, None] - gc[..., None, :], 0.0)), 0.0)
    kb = kc * bc[..., None]
    lower = jnp.where(strict, jnp.einsum('bhncd,bhnsd->bhncs', kb, kc) * decay, 0.0)
    eye = jnp.eye(CHUNK, dtype=f32)
    rhs = jnp.concatenate([vc * bc[..., None], kb * jnp.exp(gc)[..., None]], axis=-1)
    sol = lax.linalg.triangular_solve(eye + lower, rhs, left_side=True, lower=True, unit_diagonal=True)
    u, w = sol[..., :dv], sol[..., dv:]
    a_qk = jnp.einsum('bhncd,bhnsd->bhncs', qc, kc) * decay
    q_dec = qc * jnp.exp(gc)[..., None]
    k_dec = kc * jnp.exp(gc[..., -1:] - gc)[..., None]
    g_last = jnp.exp(gc[..., -1])

    def step(S, xs):
        a_i, q_i, u_i, w_i, k_i, gl = xs
        v_new = u_i - jnp.einsum('bhck,bhkv->bhcv', w_i, S)
        o = jnp.einsum('bhck,bhkv->bhcv', q_i, S) + jnp.einsum('bhcs,bhsv->bhcv', a_i, v_new)
        S = S * gl[..., None, None] + jnp.einsum('bhck,bhcv->bhkv', k_i, v_new)
        return S, o

    xs = tuple(jnp.moveaxis(a, 2, 0) for a in (a_qk, q_dec, u, w, k_dec, g_last))
    _, o = lax.scan(step, jnp.zeros((B, H, dk, dv), f32), xs)
    return o.transpose(1, 0, 3, 2, 4).reshape(B, T, H, dv)


def stick_breaking_attention(q, k, v):
    B, T, H, d = q.shape
    nb = T // QBLOCK
    scale = d ** -0.5
    kh = k.transpose(0, 2, 1, 3)
    vh = v.transpose(0, 2, 1, 3)
    qb = q.reshape(B, nb, QBLOCK, H, d).transpose(1, 0, 3, 2, 4)
    kpos = jnp.arange(T)

    def block(args):
        qi, i = args
        qpos = i * QBLOCK + jnp.arange(QBLOCK)
        mask = kpos[None, :] < qpos[:, None]
        z = jnp.einsum('bhqd,bhsd->bhqs', qi, kh).astype(jnp.float32) * scale
        log_beta = jax.nn.log_sigmoid(z)
        log_rest = jnp.where(mask, jax.nn.log_sigmoid(-z), 0.0)
        between = lax.cumsum(log_rest, axis=3, reverse=True) - log_rest
        att = jnp.where(mask, jnp.exp(log_beta + between), 0.0)
        return jnp.einsum('bhqs,bhsd->bhqd', att, vh.astype(jnp.float32)).astype(q.dtype)

    o = lax.map(block, (qb, jnp.arange(nb)))
    return o.transpose(1, 0, 3, 2, 4).reshape(B, T, H, d)


def forgetting_attention(q, k, v, log_f):
    B, T, H, d = q.shape
    nb = T // QBLOCK
    scale = d ** -0.5
    F = jnp.cumsum(log_f.astype(jnp.float32), axis=1).transpose(0, 2, 1)
    kh = k.transpose(0, 2, 1, 3)
    vh = v.transpose(0, 2, 1, 3)
    qb = q.reshape(B, nb, QBLOCK, H, d).transpose(1, 0, 3, 2, 4)
    Fq = F.reshape(B, H, nb, QBLOCK).transpose(2, 0, 1, 3)
    kpos = jnp.arange(T)

    def block(args):
        qi, fi, i = args
        qpos = i * QBLOCK + jnp.arange(QBLOCK)
        z = jnp.einsum('bhqd,bhsd->bhqs', qi, kh).astype(jnp.float32) * scale
        z = z + fi[..., :, None] - F[:, :, None, :]
        z = jnp.where(kpos[None, :] <= qpos[:, None], z, -jnp.inf)
        att = jax.nn.softmax(z, axis=-1)
        return jnp.einsum('bhqs,bhsd->bhqd', att, vh.astype(jnp.float32)).astype(q.dtype)

    o = lax.map(block, (qb, Fq, jnp.arange(nb)))
    return o.transpose(1, 0, 3, 2, 4).reshape(B, T, H, d)


def hybrid_layer(x, p_i, w_in, b_gate, conv_qkv, a_log, dt_bias, gdn_norm, conv_dw, conv_dw_bias,
                 conv_ln_g, conv_ln_b, forget_bias, w_branch, w_out, w_ple, w_ple_gate, b_ple_gate,
                 ln_g, ln_b):
    B, T, _ = x.shape
    dt = x.dtype
    proj = x @ w_in
    (qA, kA, vA, zA, aA, bA, glu, zB, qC, kC, vC, zC,
     qD, kD, vD, zD, fD, gates) = jnp.split(proj, SPLIT_POINTS, axis=-1)

    qkvA = jax.nn.silu(causal_dwconv(jnp.concatenate([qA, kA, vA], axis=-1), conv_qkv))
    qA, kA, vA = jnp.split(qkvA, 3, axis=-1)
    qA = l2_normalize(qA.reshape(B, T, GDN_HEADS, GDN_HEAD_DIM))
    kA = l2_normalize(kA.reshape(B, T, GDN_HEADS, GDN_HEAD_DIM))
    vA = vA.reshape(B, T, GDN_HEADS, GDN_HEAD_DIM)
    gA = -jnp.exp(a_log.astype(jnp.float32)) * jax.nn.softplus(aA.astype(jnp.float32) + dt_bias.astype(jnp.float32))
    betaA = jax.nn.sigmoid(bA.astype(jnp.float32))
    oA = gated_delta_rule(qA, kA, vA, gA, betaA)
    yA = rms_norm(oA, gdn_norm).reshape(B, T, GDN_WIDTH).astype(dt) * jax.nn.silu(zA)

    g_lin, g_gate = jnp.split(glu, 2, axis=-1)
    hB = g_lin * jax.nn.sigmoid(g_gate)
    hB = causal_dwconv(hB, conv_dw) + conv_dw_bias
    hB = jax.nn.silu(layer_norm(hB, conv_ln_g, conv_ln_b))
    yB = hB * jax.nn.silu(zB)

    oC = stick_breaking_attention(qC.reshape(B, T, SB_HEADS, SB_HEAD_DIM),
                                  kC.reshape(B, T, SB_HEADS, SB_HEAD_DIM),
                                  vC.reshape(B, T, SB_HEADS, SB_HEAD_DIM))
    yC = oC.reshape(B, T, SB_WIDTH) * jax.nn.silu(zC)

    log_f = jax.nn.log_sigmoid(fD.astype(jnp.float32) + forget_bias.astype(jnp.float32))
    oD = forgetting_attention(qD.reshape(B, T, FOX_HEADS, FOX_HEAD_DIM),
                              kD.reshape(B, T, FOX_HEADS, FOX_HEAD_DIM),
                              vD.reshape(B, T, FOX_HEADS, FOX_HEAD_DIM), log_f)
    yD = oD.reshape(B, T, FOX_WIDTH) * jax.nn.silu(zD)

    gate = jax.nn.sigmoid(gates + b_gate).reshape(B, T, N_BRANCH, D_MODEL)
    merged = gate[:, :, 0] * (yA @ w_branch[0])
    merged = merged + gate[:, :, 1] * (yB @ w_branch[1])
    merged = merged + gate[:, :, 2] * (yC @ w_branch[2])
    merged = merged + gate[:, :, 3] * (yD @ w_branch[3])
    mix = merged @ w_out

    r = ALPHA * x + mix
    r = r + jax.nn.sigmoid(r @ w_ple_gate + b_ple_gate) * (p_i @ w_ple)
    return layer_norm(r, ln_g, ln_b)


def setup_inputs(seed: int = 0) -> dict:
    key = jax.random.key(seed)
    ks = jax.random.split(key, 24)
    f32 = jnp.float32

    def nrm(k, shape, s):
        return jax.random.normal(k, shape, f32) * s

    dt_init = jnp.exp(jax.random.uniform(ks[6], (DEPTH, GDN_HEADS), f32, math.log(1e-3), math.log(1e-1)))
    return {
        "x": nrm(ks[0], (BATCH, SEQ, D_MODEL), 1.0),
        "p": nrm(ks[1], (DEPTH, BATCH, SEQ, PLE_DIM), 1.0),
        "w_in": nrm(ks[2], (DEPTH, D_MODEL, PROJ_WIDTH), D_MODEL ** -0.5),
        "b_gate": nrm(ks[3], (DEPTH, N_BRANCH * D_MODEL), 0.02),
        "conv_qkv": nrm(ks[4], (DEPTH, GDN_CONV, 3 * GDN_WIDTH), GDN_CONV ** -0.5),
        "a_log": jnp.log(jax.random.uniform(ks[5], (DEPTH, GDN_HEADS), f32, 1.0, 16.0)),
        "dt_bias": dt_init + jnp.log(-jnp.expm1(-dt_init)),
        "gdn_norm": 1.0 + nrm(ks[7], (DEPTH, GDN_HEAD_DIM), 0.02),
        "conv_dw": nrm(ks[8], (DEPTH, CONV_K, CONV_WIDTH), CONV_K ** -0.5),
        "conv_dw_bias": nrm(ks[9], (DEPTH, CONV_WIDTH), 0.02),
        "conv_ln_g": 1.0 + nrm(ks[10], (DEPTH, CONV_WIDTH), 0.02),
        "conv_ln_b": nrm(ks[11], (DEPTH, CONV_WIDTH), 0.02),
        "forget_bias": 2.0 + nrm(ks[12], (DEPTH, FOX_HEADS), 0.5),
        "w_branch": nrm(ks[13], (DEPTH, N_BRANCH, BRANCH_WIDTH, D_MODEL), BETA * BRANCH_WIDTH ** -0.5),
        "w_out": nrm(ks[14], (DEPTH, D_MODEL, D_MODEL), BETA * D_MODEL ** -0.5),
        "w_ple": nrm(ks[15], (DEPTH, PLE_DIM, D_MODEL), PLE_DIM ** -0.5),
        "w_ple_gate": nrm(ks[16], (DEPTH, D_MODEL, D_MODEL), D_MODEL ** -0.5),
        "b_ple_gate": nrm(ks[17], (DEPTH, D_MODEL), 0.02),
        "ln_g": 1.0 + nrm(ks[18], (DEPTH, D_MODEL), 0.02),
        "ln_b": nrm(ks[19], (DEPTH, D_MODEL), 0.02),
    }


def reference(x, p, w_in, b_gate, conv_qkv, a_log, dt_bias, gdn_norm, conv_dw, conv_dw_bias,
              conv_ln_g, conv_ln_b, forget_bias, w_branch, w_out, w_ple, w_ple_gate, b_ple_gate,
              ln_g, ln_b):
    for i in range(DEPTH):
        x = hybrid_layer(x, p[i], w_in[i], b_gate[i], conv_qkv[i], a_log[i], dt_bias[i], gdn_norm[i],
                         conv_dw[i], conv_dw_bias[i], conv_ln_g[i], conv_ln_b[i], forget_bias[i],
                         w_branch[i], w_out[i], w_ple[i], w_ple_gate[i], b_ple_gate[i], ln_g[i], ln_b[i])
    return x
```

```python
import functools
import math

import jax
import jax.numpy as jnp
from jax import lax
from jax.experimental import pallas as pl
from jax.experimental.pallas import tpu as pltpu

F32 = jnp.float32
BF16 = jnp.bfloat16

LANE = 128
EPS = 1e-5
GDN_HEADS = 4
GDN_HEAD_DIM = 128
GDN_CHUNK = 128
GDN_CONV = 4
CONV_K = 31
CONV_HALO = 32
ATT_HEADS = 8
ATT_HEAD_DIM = 64
N_BRANCH = 4
BRANCH_WIDTH = 512
EXP_ZERO_BELOW = -104.0
NEG_BIG = -1e30

COL_GATES = 0
COL_A_Q, COL_A_K, COL_A_V, COL_A_Z = 32, 36, 40, 44
COL_B_LIN, COL_B_GATE, COL_B_Z = 48, 52, 56
COL_C_Q, COL_C_V, COL_C_Z = 60, 64, 68
COL_D_Q, COL_D_V, COL_D_Z = 72, 76, 80
COL_SMALL = 84
PROJ_COLS = 85 * LANE
SMALL_A, SMALL_B, SMALL_F = 0, 4, 8

VMEM_LIMIT = 48 * 1024 * 1024


def _cparams(sem):
    return pltpu.CompilerParams(dimension_semantics=sem, vmem_limit_bytes=VMEM_LIMIT)


def _sigmoid(x):
    return 1.0 / (1.0 + jnp.exp(-x))


def _silu(x):
    return x * _sigmoid(x)


def _softplus(x):
    return jnp.maximum(x, 0.0) + jnp.log1p(jnp.exp(-jnp.abs(x)))


def _dot1(a, b, dims=(((1,), (0,)), ((), ()))):
    return lax.dot_general(a.astype(BF16), b.astype(BF16), dims, preferred_element_type=F32)


def _split3(x):
    hi = x.astype(BF16)
    r1 = x - hi.astype(F32)
    mid = r1.astype(BF16)
    lo = (r1 - mid.astype(F32)).astype(BF16)
    return hi, mid, lo


def _dot_exact_lhs(a01, x):
    a = a01.astype(BF16)
    hi, mid, lo = _split3(x)
    d = lambda p: jnp.dot(a, p, preferred_element_type=F32)
    return d(hi) + d(mid) + d(lo)


def _dot_exact_rhs(x, b01):
    b = b01.astype(BF16)
    hi, mid, lo = _split3(x)
    d = lambda p: jnp.dot(p, b, preferred_element_type=F32)
    return d(hi) + d(mid) + d(lo)


def _dot3(a, b, dims=(((1,), (0,)), ((), ()))):
    a_hi = a.astype(BF16)
    a_lo = (a - a_hi.astype(F32)).astype(BF16)
    b_hi = b.astype(BF16)
    b_lo = (b - b_hi.astype(F32)).astype(BF16)
    d = lambda p, q: lax.dot_general(p, q, dims, preferred_element_type=F32)
    return d(a_hi, b_hi) + d(a_hi, b_lo) + d(a_lo, b_hi)


_NT = (((1,), (1,)), ((), ()))
_TN = (((0,), (0,)), ((), ()))


def _proj_kernel(x_ref, w_ref, o_ref):
    o_ref[...] = jnp.dot(x_ref[...].astype(BF16), w_ref[...], preferred_element_type=F32)


def _proj(x2d, w, tm, tn):
    m, k = x2d.shape
    n = w.shape[1]
    return pl.pallas_call(
        _proj_kernel,
        grid=(n // tn, m // tm),
        in_specs=[pl.BlockSpec((tm, k), lambda j, i: (i, 0)),
                  pl.BlockSpec((k, tn), lambda j, i: (0, j))],
        out_specs=pl.BlockSpec((tm, tn), lambda j, i: (i, j)),
        out_shape=jax.ShapeDtypeStruct((m, n), F32),
        compiler_params=_cparams(("arbitrary", "arbitrary")),
    )(x2d, w)


def _proj_t_kernel(w_ref, x_ref, o_ref):
    o_ref[...] = lax.dot_general(w_ref[...], x_ref[...].astype(BF16), _NT,
                                 preferred_element_type=F32)


def _proj_t(x2d, w_t, batch, seq, tt):
    k = x2d.shape[1]
    n = w_t.shape[0]
    nt = seq // tt
    return pl.pallas_call(
        _proj_t_kernel,
        grid=(batch, nt),
        in_specs=[pl.BlockSpec((n, k), lambda b, t: (0, 0)),
                  pl.BlockSpec((tt, k), lambda b, t: (b * nt + t, 0))],
        out_specs=pl.BlockSpec((None, n, tt), lambda b, t: (b, 0, t)),
        out_shape=jax.ShapeDtypeStruct((batch, n, seq), F32),
        compiler_params=_cparams(("arbitrary", "arbitrary")),
    )(w_t, x2d)


def _gdn_kernel(q_ref, k_ref, v_ref, z_ref, qh_ref, kh_ref, vh_ref, sm_ref,
                cw_ref, alog_ref, dtb_ref, gn_ref, o_ref, s_ref, cs_ref):
    t = pl.program_id(1)
    C = GDN_CHUNK
    W = GDN_HEADS * GDN_HEAD_DIM

    @pl.when(t == 0)
    def _():
        s_ref[...] = jnp.zeros_like(s_ref)

    has_prev = (t > 0).astype(F32)

    def conv_silu(x_ref, h_ref, col0):
        cs_ref[0:8, :] = h_ref[...] * has_prev
        cs_ref[8:8 + C, :] = x_ref[...]
        acc = jnp.zeros((C, W), F32)
        for j in range(GDN_CONV):
            acc = acc + cs_ref[5 + j:5 + j + C, :] * cw_ref[j:j + 1, col0:col0 + W]
        return _silu(acc)

    q_all = conv_silu(q_ref, qh_ref, 0)
    k_all = conv_silu(k_ref, kh_ref, W)
    v_all = conv_silu(v_ref, vh_ref, 2 * W)

    sm = sm_ref[...]
    g_all = -jnp.exp(alog_ref[...]) * _softplus(sm + dtb_ref[...])
    beta_all = _sigmoid(sm)

    row = lax.broadcasted_iota(jnp.int32, (C, C), 0)
    col = lax.broadcasted_iota(jnp.int32, (C, C), 1)
    incl = row >= col
    strict = row > col
    eye = (row == col).astype(F32)
    tri = incl.astype(F32)
    ones = jnp.ones((C, C), F32)

    gc_all = _dot_exact_lhs(tri, g_all)

    for h in range(GDN_HEADS):
        sl = slice(h * GDN_HEAD_DIM, (h + 1) * GDN_HEAD_DIM)
        q = q_all[:, sl]
        k = k_all[:, sl]
        v = v_all[:, sl]
        q = q * lax.rsqrt(jnp.sum(q * q, axis=-1, keepdims=True) + 1e-6)
        k = k * lax.rsqrt(jnp.sum(k * k, axis=-1, keepdims=True) + 1e-6)
        qs = q * (GDN_HEAD_DIM ** -0.5)
        gc = gc_all[:, SMALL_A + h:SMALL_A + h + 1]
        beta = beta_all[:, SMALL_B + h:SMALL_B + h + 1]
        gc_row = _dot_exact_lhs(ones, eye * gc)
        decay = jnp.where(incl, jnp.exp(jnp.where(incl, gc - gc_row, 0.0)), 0.0)
        kb = k * beta
        lower = jnp.where(strict, _dot3(kb, k, _NT) * decay, 0.0)
        n_pow = -lower
        inv = eye + n_pow
        for _ in range(int(math.log2(C)) - 1):
            n_pow = _dot3(n_pow, n_pow)
            inv = inv + _dot3(inv, n_pow)
        e_gc = jnp.exp(gc)
        u = _dot3(inv, v * beta)
        w = _dot3(inv, kb * e_gc)
        a_qk = _dot3(qs, k, _NT) * decay
        q_dec = qs * e_gc
        gc_last = gc[C - 1:C, :]
        k_dec = k * jnp.exp(gc_last - gc)
        g_last = jnp.exp(gc_last)
        s = s_ref[h]
        v_new = u - _dot3(w, s)
        o = _dot3(q_dec, s) + _dot3(a_qk, v_new)
        s_ref[h] = s * g_last + _dot3(k_dec, v_new, _TN)
        o = o * lax.rsqrt(jnp.mean(o * o, axis=-1, keepdims=True) + EPS) * gn_ref[...]
        o_ref[:, sl] = o * _silu(z_ref[:, sl])


def _gdn(proj, conv_qkv, a_log_row, dt_row, gn_row, batch, seq):
    C = GDN_CHUNK
    W = GDN_HEADS * GDN_HEAD_DIM
    nt = seq // C
    m = batch * seq
    wb = W // LANE

    def main(colblk):
        return pl.BlockSpec((C, W), lambda b, t: (b * nt + t, colblk // wb))

    def halo(colblk):
        return pl.BlockSpec((8, W), lambda b, t: (jnp.maximum((b * nt + t) * (C // 8) - 1, 0), colblk // wb))

    full = lambda shp: pl.BlockSpec(shp, lambda b, t: (0,) * len(shp))
    return pl.pallas_call(
        _gdn_kernel,
        grid=(batch, nt),
        in_specs=[main(COL_A_Q), main(COL_A_K), main(COL_A_V), main(COL_A_Z),
                  halo(COL_A_Q), halo(COL_A_K), halo(COL_A_V),
                  pl.BlockSpec((C, LANE), lambda b, t: (b * nt + t, COL_SMALL)),
                  full((GDN_CONV, 3 * W)), full((1, LANE)), full((1, LANE)), full((1, GDN_HEAD_DIM))],
        out_specs=pl.BlockSpec((C, W), lambda b, t: (b * nt + t, 0)),
        out_shape=jax.ShapeDtypeStruct((m, W), F32),
        scratch_shapes=[pltpu.VMEM((GDN_HEADS, GDN_HEAD_DIM, GDN_HEAD_DIM), F32),
                        pltpu.VMEM((C + 8, W), F32)],
        compiler_params=_cparams(("arbitrary", "arbitrary")),
    )(proj, proj, proj, proj, proj, proj, proj, proj, conv_qkv, a_log_row, dt_row, gn_row)


def _conf_kernel(lin_ref, gate_ref, z_ref, linh_ref, gateh_ref, cw_ref, cb_ref, g_ref, b_ref,
                 o_ref, cs_ref):
    t = pl.program_id(1)
    tt = lin_ref.shape[0]
    H = CONV_HALO
    has_prev = (t > 0).astype(F32)
    cs_ref[0:H, :] = linh_ref[...] * _sigmoid(gateh_ref[...]) * has_prev
    cs_ref[H:H + tt, :] = lin_ref[...] * _sigmoid(gate_ref[...])
    acc = jnp.zeros(o_ref.shape, F32) + cb_ref[...]
    off = H - (CONV_K - 1)
    for j in range(CONV_K):
        acc = acc + cs_ref[off + j:off + j + tt, :] * cw_ref[j:j + 1, :]
    mu = jnp.mean(acc, axis=-1, keepdims=True)
    xc = acc - mu
    var = jnp.mean(xc * xc, axis=-1, keepdims=True)
    hb = xc * lax.rsqrt(var + EPS) * g_ref[...] + b_ref[...]
    o_ref[...] = _silu(hb) * _silu(z_ref[...])


def _conf(proj, conv_dw, conv_dw_bias, ln_g, ln_b, batch, seq, tt):
    Wc = BRANCH_WIDTH
    nt = seq // tt
    m = batch * seq
    wb = Wc // LANE
    H = CONV_HALO

    def main(colblk):
        return pl.BlockSpec((tt, Wc), lambda b, t: (b * nt + t, colblk // wb))

    def halo(colblk):
        return pl.BlockSpec((H, Wc), lambda b, t: (jnp.maximum((b * nt + t) * (tt // H) - 1, 0), colblk // wb))

    full = lambda shp: pl.BlockSpec(shp, lambda b, t: (0,) * len(shp))
    return pl.pallas_call(
        _conf_kernel,
        grid=(batch, nt),
        in_specs=[main(COL_B_LIN), main(COL_B_GATE), main(COL_B_Z), halo(COL_B_LIN), halo(COL_B_GATE),
                  full((CONV_K, Wc)), full((1, Wc)), full((1, Wc)), full((1, Wc))],
        out_specs=pl.BlockSpec((tt, Wc), lambda b, t: (b * nt + t, 0)),
        out_shape=jax.ShapeDtypeStruct((m, Wc), F32),
        scratch_shapes=[pltpu.VMEM((tt + H, Wc), F32)],
        compiler_params=_cparams(("arbitrary", "arbitrary")),
    )(proj, proj, proj, proj, proj, conv_dw, conv_dw_bias, ln_g, ln_b)


def _sb_kernel(q_ref, kt_ref, v_ref, z_ref, o_ref, *, tq, tk):
    qi = pl.program_id(2)
    d = ATT_HEAD_DIM
    scale = d ** -0.5
    nkb = (qi + 1) * (tq // tk)
    qpos = qi * tq + lax.broadcasted_iota(jnp.int32, (tq, tk), 0)
    kidx = lax.broadcasted_iota(jnp.int32, (tq, tk), 1)
    jr = lax.broadcasted_iota(jnp.int32, (tk, tk), 0)
    sc = lax.broadcasted_iota(jnp.int32, (tk, tk), 1)
    later = (jr > sc).astype(BF16)

    for h in range(LANE // d):
        sl = slice(h * d, (h + 1) * d)
        q = (q_ref[:, sl] * scale).astype(BF16)

        def cond(st):
            i, carry, _ = st
            return jnp.logical_and(i < nkb, jnp.max(carry) > EXP_ZERO_BELOW)

        def body(st):
            i, carry, acc = st
            j = nkb - 1 - i
            k0 = pl.multiple_of(j * tk, tk)
            kt = kt_ref[sl, pl.ds(k0, tk)]
            v = v_ref[pl.ds(k0, tk), sl]
            s = jnp.dot(q, kt.astype(BF16), preferred_element_type=F32)
            mask = (kidx + k0) < qpos
            sp = jnp.log1p(jnp.exp(-jnp.abs(s)))
            log_beta = jnp.minimum(s, 0.0) - sp
            log_rest = jnp.where(mask, jnp.minimum(-s, 0.0) - sp, 0.0)
            between = _dot_exact_rhs(log_rest, later)
            att = jnp.where(mask, jnp.exp(log_beta + between + carry), 0.0)
            acc = acc + jnp.dot(att.astype(BF16), v.astype(BF16), preferred_element_type=F32)
            carry = carry + jnp.sum(log_rest, axis=-1, keepdims=True)
            return i + 1, carry, acc

        st0 = (jnp.int32(0), jnp.zeros((tq, 1), F32), jnp.zeros((tq, d), F32))
        _, _, acc = lax.while_loop(cond, body, st0)
        o_ref[:, sl] = acc * _silu(z_ref[:, sl])


def _sb_attention(proj, kt_all, batch, seq, tq, tk):
    m = batch * seq
    nq = seq // tq
    hp = ATT_HEADS * ATT_HEAD_DIM // LANE
    return pl.pallas_call(
        functools.partial(_sb_kernel, tq=tq, tk=tk),
        grid=(batch, hp, nq),
        in_specs=[pl.BlockSpec((tq, LANE), lambda b, p, i: (b * nq + i, COL_C_Q + p)),
                  pl.BlockSpec((None, LANE, seq), lambda b, p, i: (b, p, 0)),
                  pl.BlockSpec((seq, LANE), lambda b, p, i: (b, COL_C_V + p)),
                  pl.BlockSpec((tq, LANE), lambda b, p, i: (b * nq + i, COL_C_Z + p))],
        out_specs=pl.BlockSpec((tq, LANE), lambda b, p, i: (b * nq + i, p)),
        out_shape=jax.ShapeDtypeStruct((m, BRANCH_WIDTH), F32),
        compiler_params=_cparams(("arbitrary", "arbitrary", "arbitrary")),
    )(proj, kt_all, proj, proj)


def _fcum_kernel(sm_ref, fb_ref, ftok_ref, ft_ref, carry_ref):
    t = pl.program_id(1)
    tt = sm_ref.shape[0]

    @pl.when(t == 0)
    def _():
        carry_ref[...] = jnp.zeros_like(carry_ref)

    x = sm_ref[...] + fb_ref[...]
    log_f = jnp.minimum(x, 0.0) - jnp.log1p(jnp.exp(-jnp.abs(x)))
    row = lax.broadcasted_iota(jnp.int32, (tt, tt), 0)
    col = lax.broadcasted_iota(jnp.int32, (tt, tt), 1)
    f = _dot_exact_lhs((row >= col).astype(F32), log_f) + carry_ref[...]
    carry_ref[...] = f[tt - 1:tt, :]
    ftok_ref[...] = f
    ft_ref[...] = f.T[SMALL_F:SMALL_F + ATT_HEADS, :]


def _fcum(proj, fb_row, batch, seq, tt):
    nt = seq // tt
    m = batch * seq
    return pl.pallas_call(
        _fcum_kernel,
        grid=(batch, nt),
        in_specs=[pl.BlockSpec((tt, LANE), lambda b, t: (b * nt + t, COL_SMALL)),
                  pl.BlockSpec((1, LANE), lambda b, t: (0, 0))],
        out_specs=[pl.BlockSpec((tt, LANE), lambda b, t: (b * nt + t, 0)),
                   pl.BlockSpec((None, ATT_HEADS, tt), lambda b, t: (b, 0, t))],
        out_shape=[jax.ShapeDtypeStruct((m, LANE), F32),
                   jax.ShapeDtypeStruct((batch, ATT_HEADS, seq), F32)],
        scratch_shapes=[pltpu.VMEM((1, LANE), F32)],
        compiler_params=_cparams(("arbitrary", "arbitrary")),
    )(proj, fb_row)


def _fox_kernel(q_ref, kt_ref, v_ref, z_ref, fq_ref, fk_ref, o_ref, *, tq, tk):
    qi = pl.program_id(2)
    p = pl.program_id(1)
    d = ATT_HEAD_DIM
    scale = d ** -0.5
    nkb = (qi + 1) * (tq // tk)
    qpos = qi * tq + lax.broadcasted_iota(jnp.int32, (tq, tk), 0)
    kidx = lax.broadcasted_iota(jnp.int32, (tq, tk), 1)
    hpl = LANE // d
    lane = lax.broadcasted_iota(jnp.int32, (tq, LANE), 1)
    fq_all = fq_ref[...]

    for h in range(hpl):
        sl = slice(h * d, (h + 1) * d)
        q = (q_ref[:, sl] * scale).astype(BF16)
        head = p * hpl + h
        fq = jnp.sum(jnp.where(lane == SMALL_F + head, fq_all, 0.0), axis=-1, keepdims=True)

        def body(i, st):
            mx, l, acc = st
            j = nkb - 1 - i
            k0 = pl.multiple_of(j * tk, tk)
            kt = kt_ref[sl, pl.ds(k0, tk)]
            v = v_ref[pl.ds(k0, tk), sl]
            fk = fk_ref[pl.ds(head, 1), pl.ds(k0, tk)]
            s = jnp.dot(q, kt.astype(BF16), preferred_element_type=F32)
            s = s + (fq - fk)
            mask = (kidx + k0) <= qpos
            s = jnp.where(mask, s, NEG_BIG)
            m_new = jnp.maximum(mx, jnp.max(s, axis=-1, keepdims=True))
            pr = jnp.where(mask, jnp.exp(s - m_new), 0.0)
            alpha = jnp.exp(mx - m_new)
            l = alpha * l + jnp.sum(pr, axis=-1, keepdims=True)
            acc = alpha * acc + jnp.dot(pr.astype(BF16), v.astype(BF16), preferred_element_type=F32)
            return m_new, l, acc

        st0 = (jnp.full((tq, 1), NEG_BIG, F32), jnp.zeros((tq, 1), F32), jnp.zeros((tq, d), F32))
        _, l, acc = lax.fori_loop(0, nkb, body, st0)
        o_ref[:, sl] = (acc / l) * _silu(z_ref[:, sl])


def _fox_attention(proj, kt_all, f_tok, f_t, batch, seq, tq, tk):
    m = batch * seq
    nq = seq // tq
    hp = ATT_HEADS * ATT_HEAD_DIM // LANE
    return pl.pallas_call(
        functools.partial(_fox_kernel, tq=tq, tk=tk),
        grid=(batch, hp, nq),
        in_specs=[pl.BlockSpec((tq, LANE), lambda b, p, i: (b * nq + i, COL_D_Q + p)),
                  pl.BlockSpec((None, LANE, seq), lambda b, p, i: (b, hp + p, 0)),
                  pl.BlockSpec((seq, LANE), lambda b, p, i: (b, COL_D_V + p)),
                  pl.BlockSpec((tq, LANE), lambda b, p, i: (b * nq + i, COL_D_Z + p)),
                  pl.BlockSpec((tq, LANE), lambda b, p, i: (b * nq + i, 0)),
                  pl.BlockSpec((None, ATT_HEADS, seq), lambda b, p, i: (b, 0, 0))],
        out_specs=pl.BlockSpec((tq, LANE), lambda b, p, i: (b * nq + i, p)),
        out_shape=jax.ShapeDtypeStruct((m, BRANCH_WIDTH), F32),
        compiler_params=_cparams(("arbitrary", "arbitrary", "arbitrary")),
    )(proj, kt_all, proj, proj, f_tok, f_t)


def _epi_kernel(x_ref, g_ref, ya_ref, yb_ref, yc_ref, yd_ref, p_ref, bg_ref, wb_ref, wo_ref,
                wp_ref, wpg_ref, bpg_ref, lg_ref, lb_ref, o_ref, *, alpha):
    dm = x_ref.shape[1]
    merged = jnp.zeros(x_ref.shape, F32)
    for br, y_ref in enumerate((ya_ref, yb_ref, yc_ref, yd_ref)):
        cs = slice(br * dm, (br + 1) * dm)
        gate = _sigmoid(g_ref[:, cs] + bg_ref[:, cs])
        merged = merged + gate * jnp.dot(y_ref[...].astype(BF16), wb_ref[br], preferred_element_type=F32)
    mix = jnp.dot(merged.astype(BF16), wo_ref[...], preferred_element_type=F32)
    r = alpha * x_ref[...] + mix
    pg = _sigmoid(jnp.dot(r.astype(BF16), wpg_ref[...], preferred_element_type=F32) + bpg_ref[...])
    r = r + pg * jnp.dot(p_ref[...].astype(BF16), wp_ref[...], preferred_element_type=F32)
    mu = jnp.mean(r, axis=-1, keepdims=True)
    rc = r - mu
    var = jnp.mean(rc * rc, axis=-1, keepdims=True)
    o_ref[...] = rc * lax.rsqrt(var + EPS) * lg_ref[...] + lb_ref[...]


def _epilogue(x2d, proj, ya, yb, yc, yd, p2d, b_gate, w_branch, w_out, w_ple, w_ple_gate, b_ple_gate,
              ln_g, ln_b, alpha, tm):
    m, dm = x2d.shape
    gate_blk = COL_GATES * LANE // (N_BRANCH * dm)
    assert gate_blk * N_BRANCH * dm == COL_GATES * LANE
    row = lambda w: pl.BlockSpec((tm, w), lambda i: (i, 0))
    full = lambda shp: pl.BlockSpec(shp, lambda i: (0,) * len(shp))
    return pl.pallas_call(
        functools.partial(_epi_kernel, alpha=alpha),
        grid=(m // tm,),
        in_specs=[row(dm),
                  pl.BlockSpec((tm, N_BRANCH * dm), lambda i: (i, gate_blk)),
                  row(BRANCH_WIDTH), row(BRANCH_WIDTH), row(BRANCH_WIDTH), row(BRANCH_WIDTH),
                  row(p2d.shape[1]),
                  full((1, N_BRANCH * dm)), full(w_branch.shape), full(w_out.shape), full(w_ple.shape),
                  full(w_ple_gate.shape), full((1, dm)), full((1, dm)), full((1, dm))],
        out_specs=row(dm),
        out_shape=jax.ShapeDtypeStruct((m, dm), F32),
        compiler_params=_cparams(("arbitrary",)),
    )(x2d, proj, ya, yb, yc, yd, p2d, b_gate, w_branch, w_out, w_ple, w_ple_gate, b_ple_gate, ln_g, ln_b)


def _split_w_in(w_in):
    gw, cw, aw = GDN_HEADS * GDN_HEAD_DIM, BRANCH_WIDTH, ATT_HEADS * ATT_HEAD_DIM
    dm = w_in.shape[0]
    sizes = ([gw] * 4 + [GDN_HEADS, GDN_HEADS] + [2 * cw, cw] + [aw] * 4 + [aw] * 4 + [ATT_HEADS]
             + [N_BRANCH * dm])
    assert sum(sizes) == w_in.shape[1]
    pts = [sum(sizes[:i + 1]) for i in range(len(sizes) - 1)]
    (qa, ka, va, za, aa, ba, glu, zb, qc, kc, vc, zc, qd, kd, vd, zd, fd, gates) = jnp.split(w_in, pts, axis=1)
    small = jnp.concatenate([aa, ba, fd], axis=1)
    small = jnp.pad(small, ((0, 0), (0, LANE - small.shape[1])))
    w_main = jnp.concatenate([gates, qa, ka, va, za, glu, zb, qc, vc, zc, qd, vd, zd, small], axis=1)
    assert w_main.shape[1] == PROJ_COLS
    w_kt = jnp.concatenate([kc, kd], axis=1).T
    return w_main.astype(BF16), w_kt.astype(BF16)


def _lane_row(vec, offset):
    return jnp.zeros((1, LANE), F32).at[0, offset:offset + vec.shape[0]].set(vec.astype(F32))


def _layer(x2d, p2d, batch, seq, alpha, w_in, b_gate, conv_qkv, a_log, dt_bias, gdn_norm, conv_dw,
           conv_dw_bias, conv_ln_g, conv_ln_b, forget_bias, w_branch, w_out, w_ple, w_ple_gate,
           b_ple_gate, ln_g, ln_b):
    w_main, w_kt = _split_w_in(w_in)
    proj = _proj(x2d, w_main, tm=512, tn=PROJ_COLS // 5)
    kt_all = _proj_t(x2d, w_kt, batch, seq, tt=512)

    ya = _gdn(proj, conv_qkv, _lane_row(a_log, SMALL_A), _lane_row(dt_bias, SMALL_A),
              gdn_norm.reshape(1, -1), batch, seq)
    yb = _conf(proj, conv_dw, conv_dw_bias.reshape(1, -1), conv_ln_g.reshape(1, -1),
               conv_ln_b.reshape(1, -1), batch, seq, tt=256)
    yc = _sb_attention(proj, kt_all, batch, seq, tq=256, tk=128)
    f_tok, f_t = _fcum(proj, _lane_row(forget_bias, SMALL_F), batch, seq, tt=512)
    yd = _fox_attention(proj, kt_all, f_tok, f_t, batch, seq, tq=256, tk=256)
    return _epilogue(x2d, proj, ya, yb, yc, yd, p2d, b_gate.reshape(1, -1), w_branch.astype(BF16),
                     w_out.astype(BF16), w_ple.astype(BF16), w_ple_gate.astype(BF16),
                     b_ple_gate.reshape(1, -1), ln_g.reshape(1, -1), ln_b.reshape(1, -1), alpha, tm=256)


def kernel(x, p, w_in, b_gate, conv_qkv, a_log, dt_bias, gdn_norm, conv_dw, conv_dw_bias, conv_ln_g,
           conv_ln_b, forget_bias, w_branch, w_out, w_ple, w_ple_gate, b_ple_gate, ln_g, ln_b):
    batch, seq, dm = x.shape
    depth = w_in.shape[0]
    alpha = (2 * depth) ** 0.25
    x2d = x.reshape(batch * seq, dm)
    for i in range(depth):
        x2d = _layer(x2d, p[i].reshape(batch * seq, -1), batch, seq, alpha, w_in[i], b_gate[i],
                     conv_qkv[i], a_log[i], dt_bias[i], gdn_norm[i], conv_dw[i], conv_dw_bias[i],
                     conv_ln_g[i], conv_ln_b[i], forget_bias[i], w_branch[i], w_out[i], w_ple[i],
                     w_ple_gate[i], b_ple_gate[i], ln_g[i], ln_b[i])
    return x2d.reshape(batch, seq, dm)
```

```python
import functools
import math

import jax
import jax.numpy as jnp
from jax import lax
from jax.experimental import pallas as pl
from jax.experimental.pallas import tpu as pltpu

F32 = jnp.float32
BF16 = jnp.bfloat16

LANE = 128
SUBLANE = 8
EPS = 1e-5
GDN_HEADS = 4
GDN_HEAD_DIM = 128
GDN_CHUNK = 128
GDN_CONV = 4
CONV_K = 31
CONV_HALO = 32
ATT_HEADS = 8
ATT_HEAD_DIM = 64
HEADS_PER_BLOCK = LANE // ATT_HEAD_DIM
N_BRANCH = 4
BRANCH_WIDTH = 512
LOG2E = 1.4426950408889634
LN2 = 0.6931471805599453
EXP2_ZERO_BELOW = -152.0
NORM_SLACK = 1.01
NEG_BIG = -1e30

COL_GATES = 0
COL_A_Q, COL_A_K, COL_A_V, COL_A_Z = 32, 36, 40, 44
COL_B_LIN, COL_B_GATE, COL_B_Z = 48, 52, 56
COL_C_Q, COL_C_Z = 60, 64
COL_D_Q, COL_D_Z = 68, 72
COL_SMALL = 76
PROJ_COLS = 77 * LANE
SMALL_A, SMALL_B, SMALL_F = 0, 4, 8
COL_V_D = 0
COL_V_C = 8
V_COLS = (8 + 4) * LANE

VMEM_LIMIT = 48 * 1024 * 1024

TILES = dict(proj_m=512, proj_t=512, conf_t=256, fcum_t=512, att_q=256, sb_k=128, fox_k=256, epi_m=256)


def _cparams(sem):
    return pltpu.CompilerParams(dimension_semantics=sem, vmem_limit_bytes=VMEM_LIMIT)


def _sigmoid(x):
    return 1.0 / (1.0 + jnp.exp(-x))


def _silu(x):
    return x * _sigmoid(x)


def _softplus(x):
    return jnp.maximum(x, 0.0) + jnp.log1p(jnp.exp(-jnp.abs(x)))


_NN = (((1,), (0,)), ((), ()))
_NT = (((1,), (1,)), ((), ()))
_TN = (((0,), (0,)), ((), ()))


def _dot1(a, b, dims=_NN):
    return lax.dot_general(a, b, dims, preferred_element_type=F32)


def _dot_split_rhs(a, b):
    b_hi = pltpu.bitcast(pltpu.bitcast(b, jnp.uint32) & jnp.uint32(0xFFFF0000), F32)
    return _dot1(a, b_hi) + _dot1(a, b - b_hi)


def _split3(x):
    hi = x.astype(BF16)
    r1 = x - hi.astype(F32)
    mid = r1.astype(BF16)
    lo = (r1 - mid.astype(F32)).astype(BF16)
    return hi, mid, lo


def _dot_exact_lhs(a01, x):
    a = a01.astype(BF16)
    hi, mid, lo = _split3(x)
    d = lambda p: jnp.dot(a, p, preferred_element_type=F32)
    return d(hi) + d(mid) + d(lo)


def _dot_exact_rhs(x, b01):
    hi, mid, lo = _split3(x)
    d = lambda p: jnp.dot(p, b01, preferred_element_type=F32)
    return d(hi) + d(mid) + d(lo)


def _proj_kernel(x_ref, w_ref, o_ref):
    o_ref[...] = jnp.dot(x_ref[...].astype(BF16), w_ref[...], preferred_element_type=F32).astype(o_ref.dtype)


def _proj_bias_kernel(x_ref, w_ref, b_ref, o_ref):
    acc = jnp.dot(x_ref[...].astype(BF16), w_ref[...], preferred_element_type=F32)
    o_ref[...] = (acc + b_ref[...]).astype(o_ref.dtype)


def _proj(x2d, w, tn, out_dtype, bias=None):
    m, k = x2d.shape
    n = w.shape[1]
    tm = TILES["proj_m"]
    in_specs = [pl.BlockSpec((tm, k), lambda j, i: (i, 0)),
                pl.BlockSpec((k, tn), lambda j, i: (0, j))]
    args = [x2d, w]
    if bias is not None:
        in_specs.append(pl.BlockSpec((1, tn), lambda j, i: (0, j)))
        args.append(bias)
    return pl.pallas_call(
        _proj_kernel if bias is None else _proj_bias_kernel,
        grid=(n // tn, m // tm),
        in_specs=in_specs,
        out_specs=pl.BlockSpec((tm, tn), lambda j, i: (i, j)),
        out_shape=jax.ShapeDtypeStruct((m, n), out_dtype),
        compiler_params=_cparams(("arbitrary", "arbitrary")),
        name="proj",
    )(*args)


def _proj_t_kernel(w_ref, x_ref, o_ref):
    o_ref[...] = lax.dot_general(w_ref[...], x_ref[...].astype(BF16), _NT,
                                 preferred_element_type=F32).astype(o_ref.dtype)


def _proj_t(x2d, w_t, batch, seq):
    k = x2d.shape[1]
    n = w_t.shape[0]
    tt = TILES["proj_t"]
    nt = seq // tt
    return pl.pallas_call(
        _proj_t_kernel,
        grid=(batch, nt),
        in_specs=[pl.BlockSpec((n, k), lambda b, t: (0, 0)),
                  pl.BlockSpec((tt, k), lambda b, t: (b * nt + t, 0))],
        out_specs=pl.BlockSpec((None, n, tt), lambda b, t: (b, 0, t)),
        out_shape=jax.ShapeDtypeStruct((batch, n, seq), BF16),
        compiler_params=_cparams(("arbitrary", "arbitrary")),
        name="proj_t",
    )(w_t, x2d)


def _gdn_kernel(q_ref, k_ref, v_ref, z_ref, qh_ref, kh_ref, vh_ref, sm_ref,
                cw_ref, alog_ref, dtb_ref, gn_ref, o_ref, s_ref, cs_ref, qkv_ref):
    t = pl.program_id(1)
    C = GDN_CHUNK
    W = GDN_HEADS * GDN_HEAD_DIM

    @pl.when(t == 0)
    def _():
        s_ref[...] = jnp.zeros_like(s_ref)

    has_prev = (t > 0).astype(F32)

    def conv_silu(x_ref, h_ref, col0, slot):
        cs_ref[0:SUBLANE, :] = h_ref[...] * has_prev
        cs_ref[SUBLANE:SUBLANE + C, :] = x_ref[...]
        acc = jnp.zeros((C, W), F32)
        off = SUBLANE - (GDN_CONV - 1)
        for j in range(GDN_CONV):
            acc = acc + cs_ref[off + j:off + j + C, :] * cw_ref[j:j + 1, col0:col0 + W]
        qkv_ref[slot] = _silu(acc)
        return qkv_ref[slot]

    q_all = conv_silu(q_ref, qh_ref, 0, 0)
    k_all = conv_silu(k_ref, kh_ref, W, 1)
    v_all = conv_silu(v_ref, vh_ref, 2 * W, 2)

    sm = sm_ref[...]
    g_all = -jnp.exp(alog_ref[...]) * _softplus(sm + dtb_ref[...])
    beta_all = _sigmoid(sm)

    row = lax.broadcasted_iota(jnp.int32, (C, C), 0)
    col = lax.broadcasted_iota(jnp.int32, (C, C), 1)
    incl = row >= col
    strict = row > col
    eye = (row == col).astype(F32)
    tri = incl.astype(F32)
    ones = jnp.ones((C, C), F32)

    gc_all = _dot_exact_lhs(tri, g_all)

    H = range(GDN_HEADS)
    sls = [slice(h * GDN_HEAD_DIM, (h + 1) * GDN_HEAD_DIM) for h in H]
    gcs = [gc_all[:, SMALL_A + h:SMALL_A + h + 1] for h in H]
    betas = [beta_all[:, SMALL_B + h:SMALL_B + h + 1] for h in H]
    qs, ks = [], []
    for h in H:
        q = q_all[:, sls[h]]
        k = k_all[:, sls[h]]
        qs.append(q * (lax.rsqrt(jnp.sum(q * q, axis=-1, keepdims=True) + 1e-6) * (GDN_HEAD_DIM ** -0.5)))
        ks.append(k * lax.rsqrt(jnp.sum(k * k, axis=-1, keepdims=True) + 1e-6))
    gc_rows = [_dot_exact_lhs(ones, eye * gcs[h]) for h in H]
    decays = [jnp.where(incl, jnp.exp(jnp.where(incl, gcs[h] - gc_rows[h], 0.0)), 0.0) for h in H]
    kbs = [ks[h] * betas[h] for h in H]
    e_gcs = [jnp.exp(gcs[h]) for h in H]
    n_pows = [-jnp.where(strict, _dot1(kbs[h], ks[h], _NT) * decays[h], 0.0) for h in H]
    invs = [eye + n_pows[h] for h in H]
    for _ in range(int(math.log2(C)) - 1):
        n_pows = [_dot1(n_pows[h], n_pows[h]) for h in H]
        invs = [invs[h] + _dot1(invs[h], n_pows[h]) for h in H]
    sols = [_dot_split_rhs(invs[h], jnp.concatenate([v_all[:, sls[h]] * betas[h], kbs[h] * e_gcs[h]], axis=1))
            for h in H]
    a_qks = [_dot1(qs[h], ks[h], _NT) * decays[h] for h in H]
    for h in H:
        u = sols[h][:, :GDN_HEAD_DIM]
        w = sols[h][:, GDN_HEAD_DIM:]
        gc_last = gcs[h][C - 1:C, :]
        k_dec = ks[h] * jnp.exp(gc_last - gcs[h])
        s = s_ref[h]
        v_new = u - _dot1(w, s)
        o = _dot1(qs[h] * e_gcs[h], s) + _dot1(a_qks[h], v_new)
        s_ref[h] = s * jnp.exp(gc_last) + _dot1(k_dec, v_new, _TN)
        o = o * lax.rsqrt(jnp.mean(o * o, axis=-1, keepdims=True) + EPS) * gn_ref[...]
        o_ref[:, sls[h]] = o * _silu(z_ref[:, sls[h]])


def _gdn(proj, conv_qkv, a_log_row, dt_row, gn_row, batch, seq):
    C = GDN_CHUNK
    W = GDN_HEADS * GDN_HEAD_DIM
    nt = seq // C
    m = batch * seq
    wb = W // LANE

    def main(colblk):
        return pl.BlockSpec((C, W), lambda b, t: (b * nt + t, colblk // wb))

    def halo(colblk):
        return pl.BlockSpec((SUBLANE, W),
                            lambda b, t: (jnp.maximum((b * nt + t) * (C // SUBLANE) - 1, 0), colblk // wb))

    full = lambda shp: pl.BlockSpec(shp, lambda b, t: (0,) * len(shp))
    return pl.pallas_call(
        _gdn_kernel,
        grid=(batch, nt),
        in_specs=[main(COL_A_Q), main(COL_A_K), main(COL_A_V), main(COL_A_Z),
                  halo(COL_A_Q), halo(COL_A_K), halo(COL_A_V),
                  pl.BlockSpec((C, LANE), lambda b, t: (b * nt + t, COL_SMALL)),
                  full((GDN_CONV, 3 * W)), full((1, LANE)), full((1, LANE)), full((1, GDN_HEAD_DIM))],
        out_specs=pl.BlockSpec((C, W), lambda b, t: (b * nt + t, 0)),
        out_shape=jax.ShapeDtypeStruct((m, W), F32),
        scratch_shapes=[pltpu.VMEM((GDN_HEADS, GDN_HEAD_DIM, GDN_HEAD_DIM), F32),
                        pltpu.VMEM((C + SUBLANE, W), F32),
                        pltpu.VMEM((3, C, W), F32)],
        compiler_params=_cparams(("arbitrary", "arbitrary")),
        name="gdn",
    )(proj, proj, proj, proj, proj, proj, proj, proj, conv_qkv, a_log_row, dt_row, gn_row)


def _conf_kernel(lin_ref, gate_ref, z_ref, linh_ref, gateh_ref, cw_ref, cb_ref, g_ref, b_ref,
                 o_ref, cs_ref):
    t = pl.program_id(1)
    tt = lin_ref.shape[0]
    H = CONV_HALO
    has_prev = (t > 0).astype(F32)
    cs_ref[0:H, :] = linh_ref[...] * _sigmoid(gateh_ref[...]) * has_prev
    cs_ref[H:H + tt, :] = lin_ref[...] * _sigmoid(gate_ref[...])
    acc = jnp.zeros(o_ref.shape, F32) + cb_ref[...]
    off = H - (CONV_K - 1)
    for j in range(CONV_K):
        acc = acc + cs_ref[off + j:off + j + tt, :] * cw_ref[j:j + 1, :]
    mu = jnp.mean(acc, axis=-1, keepdims=True)
    xc = acc - mu
    var = jnp.mean(xc * xc, axis=-1, keepdims=True)
    hb = xc * lax.rsqrt(var + EPS) * g_ref[...] + b_ref[...]
    o_ref[...] = _silu(hb) * _silu(z_ref[...])


def _conf(proj, conv_dw, conv_dw_bias, ln_g, ln_b, batch, seq):
    Wc = BRANCH_WIDTH
    tt = TILES["conf_t"]
    nt = seq // tt
    m = batch * seq
    wb = Wc // LANE
    H = CONV_HALO

    def main(colblk):
        return pl.BlockSpec((tt, Wc), lambda b, t: (b * nt + t, colblk // wb))

    def halo(colblk):
        return pl.BlockSpec((H, Wc), lambda b, t: (jnp.maximum((b * nt + t) * (tt // H) - 1, 0), colblk // wb))

    full = lambda shp: pl.BlockSpec(shp, lambda b, t: (0,) * len(shp))
    return pl.pallas_call(
        _conf_kernel,
        grid=(batch, nt),
        in_specs=[main(COL_B_LIN), main(COL_B_GATE), main(COL_B_Z), halo(COL_B_LIN), halo(COL_B_GATE),
                  full((CONV_K, Wc)), full((1, Wc)), full((1, Wc)), full((1, Wc))],
        out_specs=pl.BlockSpec((tt, Wc), lambda b, t: (b * nt + t, 0)),
        out_shape=jax.ShapeDtypeStruct((m, Wc), F32),
        scratch_shapes=[pltpu.VMEM((tt + H, Wc), F32)],
        compiler_params=_cparams(("arbitrary", "arbitrary")),
        name="conformer_conv",
    )(proj, proj, proj, proj, proj, conv_dw, conv_dw_bias, ln_g, ln_b)


def _merge_head_lanes(accs):
    rows = accs[0].shape[0]
    lane = lax.broadcasted_iota(jnp.int32, (rows, LANE), 1)
    pairs = [jnp.where(lane < ATT_HEAD_DIM, accs[h], accs[h + 1]) for h in range(0, len(accs), HEADS_PER_BLOCK)]
    return jnp.concatenate(pairs, axis=1)


def _sb_block(qh, kt_ref, v_ref, k0, tk, carries, accs, later_ext, mfac):
    d = ATT_HEAD_DIM
    heads = range(len(qh))
    s2 = [jnp.dot(qh[h], kt_ref[h * d:(h + 1) * d, pl.ds(k0, tk)], preferred_element_type=F32) for h in heads]
    lb, lr = [], []
    for h in heads:
        lp = jnp.log(1.0 + jnp.exp2(-jnp.abs(s2[h]))) * LOG2E
        b = jnp.minimum(s2[h], 0.0) - lp
        r = b - s2[h]
        lb.append(b)
        lr.append(r if mfac is None else r * mfac)
    bx = [_dot_exact_rhs(lr[h], later_ext) for h in heads]
    att = []
    for h in heads:
        a = jnp.exp2(lb[h] + bx[h][:, :tk] + carries[h])
        att.append((a if mfac is None else a * mfac).astype(BF16))
    accs = [accs[h] + jnp.dot(att[h], v_ref[pl.ds(k0, tk), (h // HEADS_PER_BLOCK) * LANE:
                                                 (h // HEADS_PER_BLOCK + 1) * LANE],
                              preferred_element_type=F32) for h in heads]
    carries = [carries[h] + bx[h][:, tk:] for h in heads]
    return carries, accs


def _sb_kernel(q_ref, kt_ref, v_ref, z_ref, o_ref, *, tq, tk):
    qi = pl.program_id(1)
    d = ATT_HEAD_DIM
    nh = ATT_HEADS
    nsub = tq // tk
    assert tk == LANE and tq % tk == 0
    jr = lax.broadcasted_iota(jnp.int32, (tk, tk + LANE), 0)
    sc = lax.broadcasted_iota(jnp.int32, (tk, tk + LANE), 1)
    later_ext = jnp.logical_or(jr > sc, sc >= tk).astype(BF16)
    row = lax.broadcasted_iota(jnp.int32, (tq, tk), 0)
    kidx = lax.broadcasted_iota(jnp.int32, (tq, tk), 1)

    q2 = (q_ref[...] * (LOG2E * d ** -0.5)).astype(BF16)
    qh = [q2[:, h * d:(h + 1) * d] for h in range(nh)]
    carries = [jnp.zeros((tq, LANE), F32) for _ in range(nh)]
    accs = [jnp.zeros((tq, LANE), F32) for _ in range(nh)]

    for c in range(nsub - 1, -1, -1):
        k0 = pl.multiple_of(qi * tq + c * tk, tk)
        mfac = (kidx + c * tk < row).astype(F32)
        carries, accs = _sb_block(qh, kt_ref, v_ref, k0, tk, carries, accs, later_ext, mfac)

    nold = qi * nsub

    def live(cs):
        mx = cs[0]
        for c in cs[1:]:
            mx = jnp.maximum(mx, c)
        return (jnp.max(mx) > EXP2_ZERO_BELOW).astype(jnp.int32)

    def cond(st):
        return jnp.logical_and(st[0] < nold, st[1] > 0)

    def body(st):
        i = st[0]
        k0 = pl.multiple_of((nold - 1 - i) * tk, tk)
        cs, as_ = _sb_block(qh, kt_ref, v_ref, k0, tk, list(st[2:2 + nh]), list(st[2 + nh:]), later_ext, None)
        return (i + 1, live(cs), *cs, *as_)

    st = lax.while_loop(cond, body, (jnp.int32(0), live(carries), *carries, *accs))
    o_ref[...] = _merge_head_lanes(st[2 + nh:]) * _silu(z_ref[...])


def _sb_attention(proj, kt_all, v_all, batch, seq):
    tq, tk = TILES["att_q"], TILES["sb_k"]
    m = batch * seq
    nq = seq // tq
    W = ATT_HEADS * ATT_HEAD_DIM
    wb = W // LANE
    once = pl.Buffered(1)
    return pl.pallas_call(
        functools.partial(_sb_kernel, tq=tq, tk=tk),
        grid=(batch, nq),
        in_specs=[pl.BlockSpec((tq, W), lambda b, i: (b * nq + i, COL_C_Q // wb)),
                  pl.BlockSpec((None, W, seq), lambda b, i: (b, 0, 0), pipeline_mode=once),
                  pl.BlockSpec((seq, W), lambda b, i: (b, COL_V_C // wb), pipeline_mode=once),
                  pl.BlockSpec((tq, W), lambda b, i: (b * nq + i, COL_C_Z // wb))],
        out_specs=pl.BlockSpec((tq, W), lambda b, i: (b * nq + i, 0)),
        out_shape=jax.ShapeDtypeStruct((m, W), F32),
        compiler_params=_cparams(("arbitrary", "arbitrary")),
        name="stick_breaking_attention",
    )(proj, kt_all, v_all, proj)


def _fcum_kernel(sm_ref, fb_ref, ft_ref, pm_ref, carry_ref, pmin_ref, *, blk):
    t = pl.program_id(1)
    tt = sm_ref.shape[0]

    @pl.when(t == 0)
    def _():
        carry_ref[...] = jnp.zeros_like(carry_ref)
        pmin_ref[...] = jnp.zeros_like(pmin_ref)

    x = sm_ref[...] + fb_ref[...]
    log_f = jnp.minimum(x, 0.0) - jnp.log1p(jnp.exp(-jnp.abs(x)))
    row = lax.broadcasted_iota(jnp.int32, (tt, tt), 0)
    col = lax.broadcasted_iota(jnp.int32, (tt, tt), 1)
    f = _dot_exact_lhs((row >= col).astype(F32), log_f) + carry_ref[...]
    carry_ref[...] = f[tt - 1:tt, :]
    ft_ref[...] = f.T[SMALL_F:SMALL_F + ATT_HEADS, :]
    rid = lax.broadcasted_iota(jnp.int32, (SUBLANE, LANE), 0)
    pm = pmin_ref[...]
    out = jnp.zeros((SUBLANE, LANE), F32)
    for r in range(tt // blk):
        pm = jnp.minimum(pm, jnp.min(f[r * blk:(r + 1) * blk], axis=0, keepdims=True))
        out = jnp.where(rid == r, pm, out)
    pmin_ref[...] = pm
    pm_ref[...] = out


def _fcum(proj, fb_row, batch, seq):
    tt, blk = TILES["fcum_t"], TILES["fox_k"]
    nt = seq // tt
    per = tt // blk
    ft, pm = pl.pallas_call(
        functools.partial(_fcum_kernel, blk=blk),
        grid=(batch, nt),
        in_specs=[pl.BlockSpec((tt, LANE), lambda b, t: (b * nt + t, COL_SMALL)),
                  pl.BlockSpec((1, LANE), lambda b, t: (0, 0))],
        out_specs=[pl.BlockSpec((None, ATT_HEADS, tt), lambda b, t: (b, 0, t)),
                   pl.BlockSpec((None, SUBLANE, LANE), lambda b, t: (b * nt + t, 0, 0))],
        out_shape=[jax.ShapeDtypeStruct((batch, ATT_HEADS, seq), F32),
                   jax.ShapeDtypeStruct((batch * nt, SUBLANE, LANE), F32)],
        scratch_shapes=[pltpu.VMEM((1, LANE), F32), pltpu.VMEM((1, LANE), F32)],
        compiler_params=_cparams(("arbitrary", "arbitrary")),
        name="forget_cumsum",
    )(proj, fb_row)
    pm = pm[:, :per, SMALL_F:SMALL_F + ATT_HEADS].reshape(batch, nt * per, ATT_HEADS)
    return ft, jnp.transpose(pm, (0, 2, 1)).reshape(-1)


def _fox_block(qh, kt_ref, v_ref, fk_ref, cref, k0, tk, ms, accs, mask):
    d = ATT_HEAD_DIM
    heads = range(len(qh))
    z = []
    for h in heads:
        fk2 = (fk_ref[h:h + 1, pl.ds(k0, tk)] - cref[h]) * LOG2E
        zh = jnp.dot(qh[h], kt_ref[h * d:(h + 1) * d, pl.ds(k0, tk)], preferred_element_type=F32) - fk2
        z.append(zh if mask is None else jnp.where(mask, zh, NEG_BIG))
    m_new = [jnp.maximum(ms[h], jnp.max(z[h], axis=-1, keepdims=True)) for h in heads]
    pr = [jnp.exp2(z[h] - m_new[h]).astype(BF16) for h in heads]
    accs = [jnp.exp2(ms[h] - m_new[h]) * accs[h]
            + jnp.dot(pr[h], v_ref[pl.ds(k0, tk), h * LANE:(h + 1) * LANE], preferred_element_type=F32)
            for h in heads]
    return m_new, accs


def _fox_kernel(pm_ref, q_ref, kt_ref, v_ref, z_ref, fk_ref, o_ref, kmax_ref, *, tq, tk, seq):
    b = pl.program_id(0)
    qi = pl.program_id(1)
    d = ATT_HEAD_DIM
    nh = ATT_HEADS
    nblk = seq // tk
    assert tq == tk

    @pl.when(qi == 0)
    def _():
        step = 2048 if seq % 2048 == 0 else tk
        for h in range(nh):
            best = jnp.zeros((1, step), F32)
            for c0 in range(0, seq, step):
                kf = kt_ref[h * d:(h + 1) * d, c0:c0 + step].astype(F32)
                best = jnp.maximum(best, jnp.sum(kf * kf, axis=0, keepdims=True))
            kmax_ref[h] = jnp.broadcast_to(jnp.sqrt(jnp.max(best, axis=-1, keepdims=True)), (SUBLANE, LANE))

    q2 = (q_ref[...] * (LOG2E * d ** -0.5)).astype(BF16)
    diag_mask = (lax.broadcasted_iota(jnp.int32, (tq, tk), 1) <= lax.broadcasted_iota(jnp.int32, (tq, tk), 0))

    qh, ub, base, cref = [], [], [], []
    for h in range(nh):
        qh.append(q2[:, h * d:(h + 1) * d])
        qf = qh[h].astype(F32)
        qn = jnp.sqrt(jnp.sum(qf * qf, axis=-1, keepdims=True))
        ub.append(qn * kmax_ref[h, 0:1, 0:1] * NORM_SLACK)
        base.append((b * nh + h) * nblk)
        cref.append(jnp.where(qi > 0, pm_ref[base[h] + jnp.maximum(qi - 1, 0)], 0.0))

    ms = [jnp.full((tq, 1), NEG_BIG, F32) for _ in range(nh)]
    accs = [jnp.zeros((tq, LANE), F32) for _ in range(nh)]
    ms, accs = _fox_block(qh, kt_ref, v_ref, fk_ref, cref, pl.multiple_of(qi * tk, tk), tk, ms, accs, diag_mask)

    def live(ms, j):
        jj = jnp.maximum(j, 0)
        w = None
        for h in range(nh):
            g = (cref[h] - pm_ref[base[h] + jj]) * LOG2E
            t = ub[h] - ms[h] + g
            w = t if w is None else jnp.maximum(w, t)
        return (jnp.max(w) > EXP2_ZERO_BELOW).astype(jnp.int32)

    def cond(st):
        return jnp.logical_and(st[0] < qi, st[1] > 0)

    def body(st):
        i = st[0]
        j = qi - 1 - i
        ms, accs = _fox_block(qh, kt_ref, v_ref, fk_ref, cref, pl.multiple_of(j * tk, tk), tk,
                              list(st[2:2 + nh]), list(st[2 + nh:]), None)
        return (i + 1, live(ms, j - 1), *ms, *accs)

    st = lax.while_loop(cond, body, (jnp.int32(0), live(ms, qi - 1), *ms, *accs))
    outs = [a / pltpu.roll(a, d, 1) for a in st[2 + nh:]]
    o_ref[...] = _merge_head_lanes(outs) * _silu(z_ref[...])


def _fox_attention(proj, kt_all, v_all, f_t, pm_tab, batch, seq):
    tq, tk = TILES["att_q"], TILES["fox_k"]
    m = batch * seq
    nq = seq // tq
    W = ATT_HEADS * ATT_HEAD_DIM
    wb = W // LANE
    assert COL_V_D == 0
    once = pl.Buffered(1)
    return pl.pallas_call(
        functools.partial(_fox_kernel, tq=tq, tk=tk, seq=seq),
        grid=(batch, nq),
        in_specs=[pl.BlockSpec(memory_space=pltpu.SMEM),
                  pl.BlockSpec((tq, W), lambda b, i: (b * nq + i, COL_D_Q // wb)),
                  pl.BlockSpec((None, W, seq), lambda b, i: (b, 1, 0), pipeline_mode=once),
                  pl.BlockSpec((seq, ATT_HEADS * LANE), lambda b, i: (b, 0), pipeline_mode=once),
                  pl.BlockSpec((tq, W), lambda b, i: (b * nq + i, COL_D_Z // wb)),
                  pl.BlockSpec((None, ATT_HEADS, seq), lambda b, i: (b, 0, 0), pipeline_mode=once)],
        out_specs=pl.BlockSpec((tq, W), lambda b, i: (b * nq + i, 0)),
        out_shape=jax.ShapeDtypeStruct((m, W), F32),
        scratch_shapes=[pltpu.VMEM((ATT_HEADS, SUBLANE, LANE), F32)],
        compiler_params=_cparams(("arbitrary", "arbitrary")),
        name="forgetting_attention",
    )(pm_tab, proj, kt_all, v_all, proj, f_t)


def _epi_kernel(x_ref, g_ref, ya_ref, yb_ref, yc_ref, yd_ref, p_ref, bg_ref, wb_ref, wo_ref,
                wp_ref, wpg_ref, bpg_ref, lg_ref, lb_ref, o_ref, *, alpha):
    dm = x_ref.shape[1]
    merged = jnp.zeros(x_ref.shape, F32)
    for br, y_ref in enumerate((ya_ref, yb_ref, yc_ref, yd_ref)):
        cs = slice(br * dm, (br + 1) * dm)
        gate = _sigmoid(g_ref[:, cs] + bg_ref[:, cs])
        merged = merged + gate * jnp.dot(y_ref[...].astype(BF16), wb_ref[br], preferred_element_type=F32)
    mix = jnp.dot(merged.astype(BF16), wo_ref[...], preferred_element_type=F32)
    r = alpha * x_ref[...] + mix
    pg = _sigmoid(jnp.dot(r.astype(BF16), wpg_ref[...], preferred_element_type=F32) + bpg_ref[...])
    r = r + pg * jnp.dot(p_ref[...].astype(BF16), wp_ref[...], preferred_element_type=F32)
    mu = jnp.mean(r, axis=-1, keepdims=True)
    rc = r - mu
    var = jnp.mean(rc * rc, axis=-1, keepdims=True)
    o_ref[...] = rc * lax.rsqrt(var + EPS) * lg_ref[...] + lb_ref[...]


def _epilogue(x2d, proj, ya, yb, yc, yd, p2d, b_gate, w_branch, w_out, w_ple, w_ple_gate, b_ple_gate,
              ln_g, ln_b, alpha):
    m, dm = x2d.shape
    tm = TILES["epi_m"]
    assert COL_GATES == 0
    row = lambda w: pl.BlockSpec((tm, w), lambda i: (i, 0))
    full = lambda shp: pl.BlockSpec(shp, lambda i: (0,) * len(shp))
    return pl.pallas_call(
        functools.partial(_epi_kernel, alpha=alpha),
        grid=(m // tm,),
        in_specs=[row(dm), row(N_BRANCH * dm),
                  row(BRANCH_WIDTH), row(BRANCH_WIDTH), row(BRANCH_WIDTH), row(BRANCH_WIDTH),
                  row(p2d.shape[1]),
                  full((1, N_BRANCH * dm)), full(w_branch.shape), full(w_out.shape), full(w_ple.shape),
                  full(w_ple_gate.shape), full((1, dm)), full((1, dm)), full((1, dm))],
        out_specs=row(dm),
        out_shape=jax.ShapeDtypeStruct((m, dm), F32),
        compiler_params=_cparams(("arbitrary",)),
        name="epilogue",
    )(x2d, proj, ya, yb, yc, yd, p2d, b_gate, w_branch, w_out, w_ple, w_ple_gate, b_ple_gate, ln_g, ln_b)


def _split_w_in(w_in):
    gw, cw, aw = GDN_HEADS * GDN_HEAD_DIM, BRANCH_WIDTH, ATT_HEADS * ATT_HEAD_DIM
    d = ATT_HEAD_DIM
    dm = w_in.shape[0]
    sizes = ([gw] * 4 + [GDN_HEADS, GDN_HEADS] + [2 * cw, cw] + [aw] * 4 + [aw] * 4 + [ATT_HEADS]
             + [N_BRANCH * dm])
    assert sum(sizes) == w_in.shape[1]
    pts = [sum(sizes[:i + 1]) for i in range(len(sizes) - 1)]
    (qa, ka, va, za, aa, ba, glu, zb, qc, kc, vc, zc, qd, kd, vd, zd, fd, gates) = jnp.split(w_in, pts, axis=1)
    small = jnp.concatenate([aa, ba, fd], axis=1)
    small = jnp.pad(small, ((0, 0), (0, LANE - small.shape[1])))
    w_main = jnp.concatenate([gates, qa, ka, va, za, glu, zb, qc, zc, qd, zd, small], axis=1)
    assert w_main.shape[1] == PROJ_COLS
    w_kt = jnp.concatenate([kc, kd], axis=1).T
    zeros = jnp.zeros((dm, d), w_in.dtype)
    vd_blocks, bias_blocks = [], []
    for h in range(ATT_HEADS):
        vh = vd[:, h * d:(h + 1) * d]
        pair = [vh, zeros] if h % 2 == 0 else [zeros, vh]
        ones = [jnp.zeros((d,), F32), jnp.ones((d,), F32)]
        vd_blocks += pair
        bias_blocks += ones if h % 2 == 0 else ones[::-1]
    w_v = jnp.concatenate(vd_blocks + [vc], axis=1)
    b_v = jnp.concatenate(bias_blocks + [jnp.zeros((aw,), F32)]).reshape(1, -1)
    assert w_v.shape[1] == V_COLS
    return w_main.astype(BF16), w_kt.astype(BF16), w_v.astype(BF16), b_v


def _lane_row(vec, offset):
    return jnp.zeros((1, LANE), F32).at[0, offset:offset + vec.shape[0]].set(vec.astype(F32))


def _layer(x2d, p2d, batch, seq, alpha, w_in, b_gate, conv_qkv, a_log, dt_bias, gdn_norm, conv_dw,
           conv_dw_bias, conv_ln_g, conv_ln_b, forget_bias, w_branch, w_out, w_ple, w_ple_gate,
           b_ple_gate, ln_g, ln_b):
    w_main, w_kt, w_v, b_v = _split_w_in(w_in)
    proj = _proj(x2d, w_main, PROJ_COLS // 7, F32)
    v_all = _proj(x2d, w_v, V_COLS, BF16, bias=b_v)
    kt_all = _proj_t(x2d, w_kt, batch, seq)

    ya = _gdn(proj, conv_qkv, _lane_row(a_log, SMALL_A), _lane_row(dt_bias, SMALL_A),
              gdn_norm.reshape(1, -1), batch, seq)
    yb = _conf(proj, conv_dw, conv_dw_bias.reshape(1, -1), conv_ln_g.reshape(1, -1),
               conv_ln_b.reshape(1, -1), batch, seq)
    yc = _sb_attention(proj, kt_all, v_all, batch, seq)
    f_t, pm_tab = _fcum(proj, _lane_row(forget_bias, SMALL_F), batch, seq)
    yd = _fox_attention(proj, kt_all, v_all, f_t, pm_tab, batch, seq)
    return _epilogue(x2d, proj, ya, yb, yc, yd, p2d, b_gate.reshape(1, -1), w_branch.astype(BF16),
                     w_out.astype(BF16), w_ple.astype(BF16), w_ple_gate.astype(BF16),
                     b_ple_gate.reshape(1, -1), ln_g.reshape(1, -1), ln_b.reshape(1, -1), alpha)


def kernel(x, p, w_in, b_gate, conv_qkv, a_log, dt_bias, gdn_norm, conv_dw, conv_dw_bias, conv_ln_g,
           conv_ln_b, forget_bias, w_branch, w_out, w_ple, w_ple_gate, b_ple_gate, ln_g, ln_b):
    batch, seq, dm = x.shape
    depth = w_in.shape[0]
    alpha = (2 * depth) ** 0.25
    x2d = x.reshape(batch * seq, dm)
    for i in range(depth):
        x2d = _layer(x2d, p[i].reshape(batch * seq, -1), batch, seq, alpha, w_in[i], b_gate[i],
                     conv_qkv[i], a_log[i], dt_bias[i], gdn_norm[i], conv_dw[i], conv_dw_bias[i],
                     conv_ln_g[i], conv_ln_b[i], forget_bias[i], w_branch[i], w_out[i], w_ple[i],
                     w_ple_gate[i], b_ple_gate[i], ln_g[i], ln_b[i])
    return x2d.reshape(batch, seq, dm)
```

```python
import functools
import math

import jax
import jax.numpy as jnp
from jax import lax
from jax.experimental import pallas as pl
from jax.experimental.pallas import tpu as pltpu

F32 = jnp.float32
BF16 = jnp.bfloat16

LANE = 128
SUBLANE = 8
EPS = 1e-5
GDN_HEADS = 4
GDN_HEAD_DIM = 128
GDN_CHUNK = 128
GDN_CONV = 4
CONV_K = 31
CONV_HALO = 32
ATT_HEADS = 8
ATT_HEAD_DIM = 64
HEADS_PER_BLOCK = LANE // ATT_HEAD_DIM
N_BRANCH = 4
BRANCH_WIDTH = 512
LOG2E = 1.4426950408889634
LN2 = 0.6931471805599453
EXP2_ZERO_BELOW = -152.0
NORM_SLACK = 1.01
NEG_BIG = -1e30

COL_A_Q, COL_A_K, COL_A_V, COL_A_Z = 0, 4, 8, 12
COL_B_LIN, COL_B_GATE, COL_B_Z = 16, 20, 24
COL_C_Q, COL_C_Z = 28, 32
COL_D_Q, COL_D_Z = 36, 40
COL_SMALL = 44
PROJ_COLS = 45 * LANE
SMALL_A, SMALL_B, SMALL_F = 0, 4, 8
COL_V_D = 0
COL_V_C = 8
V_COLS = (8 + 4) * LANE

VMEM_LIMIT = 48 * 1024 * 1024

TILES = dict(proj_m=256, gdn_t=256, conf_t=256, fcum_t=512, att_q=256, sb_k=128, fox_k=256, epi_m=256)


def _cparams(sem):
    return pltpu.CompilerParams(dimension_semantics=sem, vmem_limit_bytes=VMEM_LIMIT)


def _sigmoid(x):
    return 1.0 / (1.0 + jnp.exp(-x))


def _silu(x):
    return x * _sigmoid(x)


def _softplus(x):
    return jnp.maximum(x, 0.0) + jnp.log1p(jnp.exp(-jnp.abs(x)))


_NN = (((1,), (0,)), ((), ()))
_NT = (((1,), (1,)), ((), ()))
_TN = (((0,), (0,)), ((), ()))


def _dot1(a, b, dims=_NN):
    return lax.dot_general(a, b, dims, preferred_element_type=F32)


def _dot_split_rhs(a, b):
    b_hi = pltpu.bitcast(pltpu.bitcast(b, jnp.uint32) & jnp.uint32(0xFFFF0000), F32)
    return _dot1(a, b_hi) + _dot1(a, b - b_hi)


def _split3(x):
    hi = x.astype(BF16)
    r1 = x - hi.astype(F32)
    mid = r1.astype(BF16)
    lo = (r1 - mid.astype(F32)).astype(BF16)
    return hi, mid, lo


def _dot_exact_lhs(a01, x):
    a = a01.astype(BF16)
    hi, mid, lo = _split3(x)
    d = lambda p: jnp.dot(a, p, preferred_element_type=F32)
    return d(hi) + d(mid) + d(lo)


def _dot_sum01(x, b01):
    hi = x.astype(BF16)
    lo = (x - hi.astype(F32)).astype(BF16)
    return (jnp.dot(hi, b01, preferred_element_type=F32)
            + jnp.dot(lo, b01, preferred_element_type=F32))


def _proj_kernel(x_ref, wm_ref, wv_ref, bv_ref, wkt_ref, p_ref, v_ref, kt_ref):
    x16 = x_ref[...].astype(BF16)
    p_ref[...] = jnp.dot(x16, wm_ref[...], preferred_element_type=F32)
    v_ref[...] = (jnp.dot(x16, wv_ref[...], preferred_element_type=F32) + bv_ref[...]).astype(BF16)
    kt_ref[...] = lax.dot_general(wkt_ref[...], x16, _NT, preferred_element_type=F32).astype(BF16)


def _proj(x2d, w_main, w_v, b_v, w_kt, batch, seq):
    m, k = x2d.shape
    tm = TILES["proj_m"]
    nt = seq // tm
    once = pl.Buffered(1)
    full = lambda a: pl.BlockSpec(a.shape, lambda i: (0, 0), pipeline_mode=once)
    return pl.pallas_call(
        _proj_kernel,
        grid=(m // tm,),
        in_specs=[pl.BlockSpec((tm, k), lambda i: (i, 0)), full(w_main), full(w_v), full(b_v), full(w_kt)],
        out_specs=[pl.BlockSpec((tm, w_main.shape[1]), lambda i: (i, 0)),
                   pl.BlockSpec((tm, w_v.shape[1]), lambda i: (i, 0)),
                   pl.BlockSpec((None, w_kt.shape[0], tm), lambda i: (i // nt, 0, i % nt))],
        out_shape=[jax.ShapeDtypeStruct((m, w_main.shape[1]), F32),
                   jax.ShapeDtypeStruct((m, w_v.shape[1]), BF16),
                   jax.ShapeDtypeStruct((batch, w_kt.shape[0], seq), BF16)],
        compiler_params=_cparams(("arbitrary",)),
        name="proj",
    )(x2d, w_main, w_v, b_v, w_kt)


def _gdn_kernel(q_ref, k_ref, v_ref, z_ref, qh_ref, kh_ref, vh_ref, sm_ref,
                cw_ref, alog_ref, dtb_ref, gn_ref, o_ref, s_ref, cs_ref, qkv_ref):
    t = pl.program_id(1)
    C = GDN_CHUNK
    W = GDN_HEADS * GDN_HEAD_DIM
    tt = q_ref.shape[0]

    @pl.when(t == 0)
    def _():
        s_ref[...] = jnp.zeros_like(s_ref)

    has_prev = (t > 0).astype(F32)

    def conv_silu(x_ref, h_ref, col0, slot):
        cs_ref[0:SUBLANE, :] = h_ref[...] * has_prev
        cs_ref[SUBLANE:SUBLANE + tt, :] = x_ref[...]
        acc = jnp.zeros((tt, W), F32)
        off = SUBLANE - (GDN_CONV - 1)
        for j in range(GDN_CONV):
            acc = acc + cs_ref[off + j:off + j + tt, :] * cw_ref[j:j + 1, col0:col0 + W]
        qkv_ref[slot] = _silu(acc)
        return qkv_ref[slot]

    q_all = conv_silu(q_ref, qh_ref, 0, 0)
    k_all = conv_silu(k_ref, kh_ref, W, 1)
    v_all = conv_silu(v_ref, vh_ref, 2 * W, 2)

    sm = sm_ref[...]
    g_all = -jnp.exp(alog_ref[...]) * _softplus(sm + dtb_ref[...])
    beta_all = _sigmoid(sm)

    row = lax.broadcasted_iota(jnp.int32, (C, C), 0)
    col = lax.broadcasted_iota(jnp.int32, (C, C), 1)
    incl = row >= col
    strict = row > col
    eye = (row == col).astype(F32)
    tri = incl.astype(F32)
    ones = jnp.ones((C, C), F32)

    chunks = range(tt // C)
    units = [(c, h) for c in chunks for h in range(GDN_HEADS)]
    U = range(len(units))
    rs = [slice(c * C, (c + 1) * C) for c, _ in units]
    sls = [slice(h * GDN_HEAD_DIM, (h + 1) * GDN_HEAD_DIM) for _, h in units]
    gc_chunk = [_dot_exact_lhs(tri, g_all[c * C:(c + 1) * C]) for c in chunks]
    gcs = [gc_chunk[c][:, SMALL_A + h:SMALL_A + h + 1] for c, h in units]
    betas = [beta_all[rs[u], SMALL_B + units[u][1]:SMALL_B + units[u][1] + 1] for u in U]
    qs, ks = [], []
    for u in U:
        q = q_all[rs[u], sls[u]]
        k = k_all[rs[u], sls[u]]
        qs.append(q * (lax.rsqrt(jnp.sum(q * q, axis=-1, keepdims=True) + 1e-6) * (GDN_HEAD_DIM ** -0.5)))
        ks.append(k * lax.rsqrt(jnp.sum(k * k, axis=-1, keepdims=True) + 1e-6))
    gc_rows = [_dot_exact_lhs(ones, eye * gcs[u]) for u in U]
    decays = [jnp.where(incl, jnp.exp(jnp.where(incl, gcs[u] - gc_rows[u], 0.0)), 0.0) for u in U]
    kbs = [ks[u] * betas[u] for u in U]
    e_gcs = [jnp.exp(gcs[u]) for u in U]
    n_pows = [-jnp.where(strict, _dot1(kbs[u], ks[u], _NT) * decays[u], 0.0) for u in U]
    invs = [eye + n_pows[u] for u in U]
    for _ in range(int(math.log2(C)) - 1):
        n_pows = [_dot1(n_pows[u], n_pows[u]) for u in U]
        invs = [invs[u] + _dot1(invs[u], n_pows[u]) for u in U]
    sols = [_dot_split_rhs(invs[u], jnp.concatenate([v_all[rs[u], sls[u]] * betas[u], kbs[u] * e_gcs[u]], axis=1))
            for u in U]
    a_qks = [_dot1(qs[u], ks[u], _NT) * decays[u] for u in U]
    for u in U:
        h = units[u][1]
        usol = sols[u][:, :GDN_HEAD_DIM]
        w = sols[u][:, GDN_HEAD_DIM:]
        gc_last = gcs[u][C - 1:C, :]
        k_dec = ks[u] * jnp.exp(gc_last - gcs[u])
        s = s_ref[h]
        v_new = usol - _dot1(w, s)
        o = _dot1(qs[u] * e_gcs[u], s) + _dot1(a_qks[u], v_new)
        s_ref[h] = s * jnp.exp(gc_last) + _dot1(k_dec, v_new, _TN)
        o = o * lax.rsqrt(jnp.mean(o * o, axis=-1, keepdims=True) + EPS) * gn_ref[...]
        o_ref[rs[u], sls[u]] = o * _silu(z_ref[rs[u], sls[u]])


def _gdn(proj, conv_qkv, a_log_row, dt_row, gn_row, batch, seq):
    tt = TILES["gdn_t"]
    assert tt % GDN_CHUNK == 0
    W = GDN_HEADS * GDN_HEAD_DIM
    nt = seq // tt
    m = batch * seq
    wb = W // LANE

    def main(colblk):
        return pl.BlockSpec((tt, W), lambda b, t: (b * nt + t, colblk // wb))

    def halo(colblk):
        return pl.BlockSpec((SUBLANE, W),
                            lambda b, t: (jnp.maximum((b * nt + t) * (tt // SUBLANE) - 1, 0), colblk // wb))

    full = lambda shp: pl.BlockSpec(shp, lambda b, t: (0,) * len(shp))
    return pl.pallas_call(
        _gdn_kernel,
        grid=(batch, nt),
        in_specs=[main(COL_A_Q), main(COL_A_K), main(COL_A_V), main(COL_A_Z),
                  halo(COL_A_Q), halo(COL_A_K), halo(COL_A_V),
                  pl.BlockSpec((tt, LANE), lambda b, t: (b * nt + t, COL_SMALL)),
                  full((GDN_CONV, 3 * W)), full((1, LANE)), full((1, LANE)), full((1, GDN_HEAD_DIM))],
        out_specs=pl.BlockSpec((tt, W), lambda b, t: (b * nt + t, 0)),
        out_shape=jax.ShapeDtypeStruct((m, W), F32),
        scratch_shapes=[pltpu.VMEM((GDN_HEADS, GDN_HEAD_DIM, GDN_HEAD_DIM), F32),
                        pltpu.VMEM((tt + SUBLANE, W), F32),
                        pltpu.VMEM((3, tt, W), F32)],
        compiler_params=_cparams(("arbitrary", "arbitrary")),
        name="gdn",
    )(proj, proj, proj, proj, proj, proj, proj, proj, conv_qkv, a_log_row, dt_row, gn_row)


def _conf_kernel(lin_ref, gate_ref, z_ref, linh_ref, gateh_ref, cw_ref, cb_ref, g_ref, b_ref,
                 o_ref, cs_ref, sh_ref):
    t = pl.program_id(1)
    tt = lin_ref.shape[0]
    H = CONV_HALO
    has_prev = (t > 0).astype(F32)
    cs_ref[0:H, :] = linh_ref[...] * _sigmoid(gateh_ref[...]) * has_prev
    cs_ref[H:H + tt, :] = lin_ref[...] * _sigmoid(gate_ref[...])
    span = tt + H - SUBLANE
    for r in range(1, SUBLANE):
        sh_ref[r - 1, 0:span, :] = cs_ref[r:r + span, :]
    acc = jnp.zeros(o_ref.shape, F32) + cb_ref[...]
    off = H - (CONV_K - 1)
    for j in range(CONV_K):
        a, r = divmod(off + j, SUBLANE)
        tap = cs_ref[a * SUBLANE:a * SUBLANE + tt, :] if r == 0 else sh_ref[r - 1, a * SUBLANE:a * SUBLANE + tt, :]
        acc = acc + tap * cw_ref[j:j + 1, :]
    mu = jnp.mean(acc, axis=-1, keepdims=True)
    xc = acc - mu
    var = jnp.mean(xc * xc, axis=-1, keepdims=True)
    hb = xc * lax.rsqrt(var + EPS) * g_ref[...] + b_ref[...]
    o_ref[...] = _silu(hb) * _silu(z_ref[...])


def _conf(proj, conv_dw, conv_dw_bias, ln_g, ln_b, batch, seq):
    Wc = BRANCH_WIDTH
    tt = TILES["conf_t"]
    nt = seq // tt
    m = batch * seq
    wb = Wc // LANE
    H = CONV_HALO

    def main(colblk):
        return pl.BlockSpec((tt, Wc), lambda b, t: (b * nt + t, colblk // wb))

    def halo(colblk):
        return pl.BlockSpec((H, Wc), lambda b, t: (jnp.maximum((b * nt + t) * (tt // H) - 1, 0), colblk // wb))

    full = lambda shp: pl.BlockSpec(shp, lambda b, t: (0,) * len(shp))
    return pl.pallas_call(
        _conf_kernel,
        grid=(batch, nt),
        in_specs=[main(COL_B_LIN), main(COL_B_GATE), main(COL_B_Z), halo(COL_B_LIN), halo(COL_B_GATE),
                  full((CONV_K, Wc)), full((1, Wc)), full((1, Wc)), full((1, Wc))],
        out_specs=pl.BlockSpec((tt, Wc), lambda b, t: (b * nt + t, 0)),
        out_shape=jax.ShapeDtypeStruct((m, Wc), F32),
        scratch_shapes=[pltpu.VMEM((tt + H, Wc), F32),
                        pltpu.VMEM((SUBLANE - 1, tt + H - SUBLANE, Wc), F32)],
        compiler_params=_cparams(("arbitrary", "arbitrary")),
        name="conformer_conv",
    )(proj, proj, proj, proj, proj, conv_dw, conv_dw_bias, ln_g, ln_b)


def _merge_head_lanes(accs):
    rows = accs[0].shape[0]
    lane = lax.broadcasted_iota(jnp.int32, (rows, LANE), 1)
    pairs = [jnp.where(lane < ATT_HEAD_DIM, accs[h], accs[h + 1]) for h in range(0, len(accs), HEADS_PER_BLOCK)]
    return jnp.concatenate(pairs, axis=1)


def _sb_block(qh, kt_ref, v_ref, k0, tk, carry_ref, acc_ref, later_ext, r0, mfac):
    d = ATT_HEAD_DIM
    heads = range(len(qh))
    s2 = [jnp.dot(qh[h][r0:], kt_ref[h * d:(h + 1) * d, pl.ds(k0, tk)], preferred_element_type=F32)
          for h in heads]
    lb, lr = [], []
    for h in heads:
        lp = jnp.log(1.0 + jnp.exp2(-jnp.abs(s2[h]))) * LOG2E
        b = jnp.minimum(s2[h], 0.0) - lp
        r = b - s2[h]
        lb.append(b)
        lr.append(r if mfac is None else r * mfac)
    bx = [_dot_sum01(lr[h], later_ext[:, :tk]) for h in heads]
    rsum = [jnp.sum(lr[h], axis=-1, keepdims=True) for h in heads]
    att = []
    for h in heads:
        a = jnp.exp2(lb[h] + bx[h] + carry_ref[h, r0:, :])
        att.append((a if mfac is None else a * mfac).astype(BF16))
    for h in heads:
        blk = h // HEADS_PER_BLOCK
        acc_ref[h, r0:, :] += jnp.dot(att[h], v_ref[pl.ds(k0, tk), blk * LANE:(blk + 1) * LANE],
                                      preferred_element_type=F32)
        carry_ref[h, r0:, :] += rsum[h]


def _sb_kernel(q_ref, kt_ref, v_ref, z_ref, o_ref, carry_ref, acc_ref, *, tq, tk):
    qi = pl.program_id(1)
    d = ATT_HEAD_DIM
    nh = ATT_HEADS
    nsub = tq // tk
    assert tk == LANE and tq % tk == 0
    jr = lax.broadcasted_iota(jnp.int32, (tk, tk + LANE), 0)
    sc = lax.broadcasted_iota(jnp.int32, (tk, tk + LANE), 1)
    later_ext = jnp.logical_or(jr > sc, sc >= tk).astype(BF16)

    q2 = (q_ref[...] * (LOG2E * d ** -0.5)).astype(BF16)
    qh = [q2[:, h * d:(h + 1) * d] for h in range(nh)]
    carry_ref[...] = jnp.zeros_like(carry_ref)
    acc_ref[...] = jnp.zeros_like(acc_ref)
    block = functools.partial(_sb_block, qh, kt_ref, v_ref)

    for c in range(nsub - 1, -1, -1):
        r0 = c * tk
        row = lax.broadcasted_iota(jnp.int32, (tq - r0, tk), 0)
        kidx = lax.broadcasted_iota(jnp.int32, (tq - r0, tk), 1)
        block(pl.multiple_of(qi * tq + r0, tk), tk, carry_ref, acc_ref, later_ext, r0, (kidx < row).astype(F32))

    nold = qi * nsub

    def live():
        mx = carry_ref[0]
        for h in range(1, nh):
            mx = jnp.maximum(mx, carry_ref[h])
        return (jnp.max(mx) > EXP2_ZERO_BELOW).astype(jnp.int32)

    def cond(st):
        return jnp.logical_and(st[0] < nold, st[1] > 0)

    def body(st):
        block(pl.multiple_of((nold - 1 - st[0]) * tk, tk), tk, carry_ref, acc_ref, later_ext, 0, None)
        return st[0] + 1, live()

    lax.while_loop(cond, body, (jnp.int32(0), live()))
    o_ref[...] = _merge_head_lanes([acc_ref[h] for h in range(nh)]) * _silu(z_ref[...])


def _sb_attention(proj, kt_all, v_all, batch, seq):
    tq, tk = TILES["att_q"], TILES["sb_k"]
    m = batch * seq
    nq = seq // tq
    W = ATT_HEADS * ATT_HEAD_DIM
    wb = W // LANE
    once = pl.Buffered(1)
    return pl.pallas_call(
        functools.partial(_sb_kernel, tq=tq, tk=tk),
        grid=(batch, nq),
        in_specs=[pl.BlockSpec((tq, W), lambda b, i: (b * nq + i, COL_C_Q // wb)),
                  pl.BlockSpec((None, W, seq), lambda b, i: (b, 0, 0), pipeline_mode=once),
                  pl.BlockSpec((seq, W), lambda b, i: (b, COL_V_C // wb), pipeline_mode=once),
                  pl.BlockSpec((tq, W), lambda b, i: (b * nq + i, COL_C_Z // wb))],
        out_specs=pl.BlockSpec((tq, W), lambda b, i: (b * nq + i, 0)),
        out_shape=jax.ShapeDtypeStruct((m, W), F32),
        scratch_shapes=[pltpu.VMEM((ATT_HEADS, tq, LANE), F32), pltpu.VMEM((ATT_HEADS, tq, LANE), F32)],
        compiler_params=_cparams(("arbitrary", "arbitrary")),
        name="stick_breaking_attention",
    )(proj, kt_all, v_all, proj)


def _fcum_kernel(sm_ref, fb_ref, ft_ref, pm_ref, carry_ref, pmin_ref, *, blk):
    t = pl.program_id(1)
    tt = sm_ref.shape[0]

    @pl.when(t == 0)
    def _():
        carry_ref[...] = jnp.zeros_like(carry_ref)
        pmin_ref[...] = jnp.zeros_like(pmin_ref)

    x = sm_ref[...] + fb_ref[...]
    log_f = jnp.minimum(x, 0.0) - jnp.log1p(jnp.exp(-jnp.abs(x)))
    row = lax.broadcasted_iota(jnp.int32, (tt, tt), 0)
    col = lax.broadcasted_iota(jnp.int32, (tt, tt), 1)
    f = _dot_exact_lhs((row >= col).astype(F32), log_f) + carry_ref[...]
    carry_ref[...] = f[tt - 1:tt, :]
    ft_ref[...] = f.T[SMALL_F:SMALL_F + ATT_HEADS, :]
    rid = lax.broadcasted_iota(jnp.int32, (SUBLANE, LANE), 0)
    pm = pmin_ref[...]
    out = jnp.zeros((SUBLANE, LANE), F32)
    for r in range(tt // blk):
        pm = jnp.minimum(pm, jnp.min(f[r * blk:(r + 1) * blk], axis=0, keepdims=True))
        out = jnp.where(rid == r, pm, out)
    pmin_ref[...] = pm
    pm_ref[...] = out


def _fcum(proj, fb_row, batch, seq):
    tt, blk = TILES["fcum_t"], TILES["fox_k"]
    nt = seq // tt
    per = tt // blk
    ft, pm = pl.pallas_call(
        functools.partial(_fcum_kernel, blk=blk),
        grid=(batch, nt),
        in_specs=[pl.BlockSpec((tt, LANE), lambda b, t: (b * nt + t, COL_SMALL)),
                  pl.BlockSpec((1, LANE), lambda b, t: (0, 0))],
        out_specs=[pl.BlockSpec((None, ATT_HEADS, tt), lambda b, t: (b, 0, t)),
                   pl.BlockSpec((None, SUBLANE, LANE), lambda b, t: (b * nt + t, 0, 0))],
        out_shape=[jax.ShapeDtypeStruct((batch, ATT_HEADS, seq), F32),
                   jax.ShapeDtypeStruct((batch * nt, SUBLANE, LANE), F32)],
        scratch_shapes=[pltpu.VMEM((1, LANE), F32), pltpu.VMEM((1, LANE), F32)],
        compiler_params=_cparams(("arbitrary", "arbitrary")),
        name="forget_cumsum",
    )(proj, fb_row)
    pm = pm[:, :per, SMALL_F:SMALL_F + ATT_HEADS].reshape(batch, nt * per, ATT_HEADS)
    return ft, jnp.transpose(pm, (0, 2, 1)).reshape(-1)


def _fox_block(qh, kt_ref, v_ref, fk_ref, cref, k0, tk, m_ref, acc_ref, mode):
    d = ATT_HEAD_DIM
    heads = range(len(qh))
    pv = lambda pr, h: jnp.dot(pr, v_ref[pl.ds(k0, tk), h * LANE:(h + 1) * LANE], preferred_element_type=F32)
    z = []
    for h in heads:
        fk2 = (fk_ref[h:h + 1, pl.ds(k0, tk)] - cref[h]) * LOG2E
        z.append(jnp.dot(qh[h], kt_ref[h * d:(h + 1) * d, pl.ds(k0, tk)], preferred_element_type=F32) - fk2)
    if mode == "plain":
        pr = [jnp.exp2(z[h] - m_ref[h]).astype(BF16) for h in heads]
        for h in heads:
            acc_ref[h] += pv(pr[h], h)
        return
    if mode == "first":
        keep = (lax.broadcasted_iota(jnp.int32, z[0].shape, 1) <= lax.broadcasted_iota(jnp.int32, z[0].shape, 0))
        z = [jnp.where(keep, zh, NEG_BIG) for zh in z]
        m_new = [jnp.max(z[h], axis=-1, keepdims=True) for h in heads]
    else:
        m_old = [m_ref[h] for h in heads]
        m_new = [jnp.maximum(m_old[h], jnp.max(z[h], axis=-1, keepdims=True)) for h in heads]
    pr = [jnp.exp2(z[h] - m_new[h]).astype(BF16) for h in heads]
    for h in heads:
        if mode == "first":
            acc_ref[h] = pv(pr[h], h)
        else:
            acc_ref[h] = jnp.exp2(m_old[h] - m_new[h]) * acc_ref[h] + pv(pr[h], h)
        m_ref[h] = m_new[h]


def _fox_kernel(pm_ref, q_ref, kt_ref, v_ref, z_ref, fk_ref, o_ref, kmax_ref, m_ref, acc_ref, *, tq, tk, seq):
    b = pl.program_id(0)
    qi = pl.program_id(1)
    d = ATT_HEAD_DIM
    nh = ATT_HEADS
    nblk = seq // tk
    assert tq == tk

    @pl.when(qi == 0)
    def _():
        step = 2048 if seq % 2048 == 0 else tk
        for h in range(nh):
            best = jnp.zeros((1, step), F32)
            for c0 in range(0, seq, step):
                kf = kt_ref[h * d:(h + 1) * d, c0:c0 + step].astype(F32)
                best = jnp.maximum(best, jnp.sum(kf * kf, axis=0, keepdims=True))
            kmax_ref[h] = jnp.broadcast_to(jnp.sqrt(jnp.max(best, axis=-1, keepdims=True)), (SUBLANE, LANE))

    q2 = (q_ref[...] * (LOG2E * d ** -0.5)).astype(BF16)

    qh, ub, base, cref = [], [], [], []
    for h in range(nh):
        qh.append(q2[:, h * d:(h + 1) * d])
        qf = qh[h].astype(F32)
        qn = jnp.sqrt(jnp.sum(qf * qf, axis=-1, keepdims=True))
        ub.append(qn * kmax_ref[h, 0:1, 0:1] * NORM_SLACK)
        base.append((b * nh + h) * nblk)
        cref.append(jnp.where(qi > 0, pm_ref[base[h] + jnp.maximum(qi - 1, 0)], 0.0))

    block = functools.partial(_fox_block, qh, kt_ref, v_ref, fk_ref, cref)
    block(pl.multiple_of(qi * tk, tk), tk, m_ref, acc_ref, "first")

    def slack():
        return [jnp.max(ub[h] - m_ref[h]) for h in range(nh)]

    def excess(sl, j):
        jj = jnp.maximum(j, 0)
        w = sl[0] + (cref[0] - pm_ref[base[0] + jj]) * LOG2E
        for h in range(1, nh):
            w = jnp.maximum(w, sl[h] + (cref[h] - pm_ref[base[h] + jj]) * LOG2E)
        return w

    def cond(st):
        return jnp.logical_and(st[0] < qi, excess(st[1:1 + nh], qi - 1 - st[0]) > EXP2_ZERO_BELOW)

    def body(st):
        i = st[0]
        j = qi - 1 - i
        k0 = pl.multiple_of(j * tk, tk)
        sl = list(st[1:])

        def plain():
            block(k0, tk, m_ref, acc_ref, "plain")
            return tuple(sl)

        def online():
            block(k0, tk, m_ref, acc_ref, "online")
            return tuple(slack())

        return (i + 1, *lax.cond(excess(sl, j) <= 0.0, plain, online))

    lax.while_loop(cond, body, (jnp.int32(0), *slack()))
    outs = [acc_ref[h] / pltpu.roll(acc_ref[h], d, 1) for h in range(nh)]
    o_ref[...] = _merge_head_lanes(outs) * _silu(z_ref[...])


def _fox_attention(proj, kt_all, v_all, f_t, pm_tab, batch, seq):
    tq, tk = TILES["att_q"], TILES["fox_k"]
    m = batch * seq
    nq = seq // tq
    W = ATT_HEADS * ATT_HEAD_DIM
    wb = W // LANE
    assert COL_V_D == 0
    once = pl.Buffered(1)
    return pl.pallas_call(
        functools.partial(_fox_kernel, tq=tq, tk=tk, seq=seq),
        grid=(batch, nq),
        in_specs=[pl.BlockSpec(memory_space=pltpu.SMEM),
                  pl.BlockSpec((tq, W), lambda b, i: (b * nq + i, COL_D_Q // wb)),
                  pl.BlockSpec((None, W, seq), lambda b, i: (b, 1, 0), pipeline_mode=once),
                  pl.BlockSpec((seq, ATT_HEADS * LANE), lambda b, i: (b, 0), pipeline_mode=once),
                  pl.BlockSpec((tq, W), lambda b, i: (b * nq + i, COL_D_Z // wb)),
                  pl.BlockSpec((None, ATT_HEADS, seq), lambda b, i: (b, 0, 0), pipeline_mode=once)],
        out_specs=pl.BlockSpec((tq, W), lambda b, i: (b * nq + i, 0)),
        out_shape=jax.ShapeDtypeStruct((m, W), F32),
        scratch_shapes=[pltpu.VMEM((ATT_HEADS, SUBLANE, LANE), F32),
                        pltpu.VMEM((ATT_HEADS, tq, 1), F32),
                        pltpu.VMEM((ATT_HEADS, tq, LANE), F32)],
        compiler_params=_cparams(("arbitrary", "arbitrary")),
        name="forgetting_attention",
    )(pm_tab, proj, kt_all, v_all, proj, f_t)


def _epi_kernel(x_ref, ya_ref, yb_ref, yc_ref, yd_ref, p_ref, wg_ref, bg_ref, wb_ref, wo_ref,
                wp_ref, wpg_ref, bpg_ref, lg_ref, lb_ref, o_ref, *, alpha):
    dm = x_ref.shape[1]
    x16 = x_ref[...].astype(BF16)
    merged = jnp.zeros(x_ref.shape, F32)
    for br, y_ref in enumerate((ya_ref, yb_ref, yc_ref, yd_ref)):
        cs = slice(br * dm, (br + 1) * dm)
        gate = _sigmoid(jnp.dot(x16, wg_ref[:, cs], preferred_element_type=F32) + bg_ref[:, cs])
        merged = merged + gate * jnp.dot(y_ref[...].astype(BF16), wb_ref[br], preferred_element_type=F32)
    mix = jnp.dot(merged.astype(BF16), wo_ref[...], preferred_element_type=F32)
    r = alpha * x_ref[...] + mix
    pg = _sigmoid(jnp.dot(r.astype(BF16), wpg_ref[...], preferred_element_type=F32) + bpg_ref[...])
    r = r + pg * jnp.dot(p_ref[...].astype(BF16), wp_ref[...], preferred_element_type=F32)
    mu = jnp.mean(r, axis=-1, keepdims=True)
    rc = r - mu
    var = jnp.mean(rc * rc, axis=-1, keepdims=True)
    o_ref[...] = rc * lax.rsqrt(var + EPS) * lg_ref[...] + lb_ref[...]


def _epilogue(x2d, ya, yb, yc, yd, p2d, w_gates, b_gate, w_branch, w_out, w_ple, w_ple_gate, b_ple_gate,
              ln_g, ln_b, alpha):
    m, dm = x2d.shape
    tm = TILES["epi_m"]
    row = lambda w: pl.BlockSpec((tm, w), lambda i: (i, 0))
    once = pl.Buffered(1)
    full = lambda shp: pl.BlockSpec(shp, lambda i: (0,) * len(shp), pipeline_mode=once)
    return pl.pallas_call(
        functools.partial(_epi_kernel, alpha=alpha),
        grid=(m // tm,),
        in_specs=[row(dm),
                  row(BRANCH_WIDTH), row(BRANCH_WIDTH), row(BRANCH_WIDTH), row(BRANCH_WIDTH),
                  row(p2d.shape[1]),
                  full(w_gates.shape), full((1, N_BRANCH * dm)), full(w_branch.shape), full(w_out.shape),
                  full(w_ple.shape), full(w_ple_gate.shape), full((1, dm)), full((1, dm)), full((1, dm))],
        out_specs=row(dm),
        out_shape=jax.ShapeDtypeStruct((m, dm), F32),
        compiler_params=_cparams(("arbitrary",)),
        name="epilogue",
    )(x2d, ya, yb, yc, yd, p2d, w_gates, b_gate, w_branch, w_out, w_ple, w_ple_gate, b_ple_gate, ln_g, ln_b)


def _split_w_in(w_in):
    gw, cw, aw = GDN_HEADS * GDN_HEAD_DIM, BRANCH_WIDTH, ATT_HEADS * ATT_HEAD_DIM
    d = ATT_HEAD_DIM
    dm = w_in.shape[0]
    sizes = ([gw] * 4 + [GDN_HEADS, GDN_HEADS] + [2 * cw, cw] + [aw] * 4 + [aw] * 4 + [ATT_HEADS]
             + [N_BRANCH * dm])
    assert sum(sizes) == w_in.shape[1]
    pts = [sum(sizes[:i + 1]) for i in range(len(sizes) - 1)]
    (qa, ka, va, za, aa, ba, glu, zb, qc, kc, vc, zc, qd, kd, vd, zd, fd, gates) = jnp.split(w_in, pts, axis=1)
    small = jnp.concatenate([aa, ba, fd], axis=1)
    small = jnp.pad(small, ((0, 0), (0, LANE - small.shape[1])))
    w_main = jnp.concatenate([qa, ka, va, za, glu, zb, qc, zc, qd, zd, small], axis=1)
    assert w_main.shape[1] == PROJ_COLS
    w_kt = jnp.concatenate([kc, kd], axis=1).T
    zeros = jnp.zeros((dm, d), w_in.dtype)
    vd_blocks, bias_blocks = [], []
    for h in range(ATT_HEADS):
        vh = vd[:, h * d:(h + 1) * d]
        pair = [vh, zeros] if h % 2 == 0 else [zeros, vh]
        ones = [jnp.zeros((d,), F32), jnp.ones((d,), F32)]
        vd_blocks += pair
        bias_blocks += ones if h % 2 == 0 else ones[::-1]
    w_v = jnp.concatenate(vd_blocks + [vc], axis=1)
    b_v = jnp.concatenate(bias_blocks + [jnp.zeros((aw,), F32)]).reshape(1, -1)
    assert w_v.shape[1] == V_COLS
    return w_main.astype(BF16), w_kt.astype(BF16), w_v.astype(BF16), b_v, gates.astype(BF16)


def _lane_row(vec, offset):
    return jnp.zeros((1, LANE), F32).at[0, offset:offset + vec.shape[0]].set(vec.astype(F32))


def _layer(x2d, p2d, batch, seq, alpha, w_in, b_gate, conv_qkv, a_log, dt_bias, gdn_norm, conv_dw,
           conv_dw_bias, conv_ln_g, conv_ln_b, forget_bias, w_branch, w_out, w_ple, w_ple_gate,
           b_ple_gate, ln_g, ln_b):
    w_main, w_kt, w_v, b_v, w_gates = _split_w_in(w_in)
    proj, v_all, kt_all = _proj(x2d, w_main, w_v, b_v, w_kt, batch, seq)

    ya = _gdn(proj, conv_qkv, _lane_row(a_log, SMALL_A), _lane_row(dt_bias, SMALL_A),
              gdn_norm.reshape(1, -1), batch, seq)
    yb = _conf(proj, conv_dw, conv_dw_bias.reshape(1, -1), conv_ln_g.reshape(1, -1),
               conv_ln_b.reshape(1, -1), batch, seq)
    yc = _sb_attention(proj, kt_all, v_all, batch, seq)
    f_t, pm_tab = _fcum(proj, _lane_row(forget_bias, SMALL_F), batch, seq)
    yd = _fox_attention(proj, kt_all, v_all, f_t, pm_tab, batch, seq)
    return _epilogue(x2d, ya, yb, yc, yd, p2d, w_gates, b_gate.reshape(1, -1), w_branch.astype(BF16),
                     w_out.astype(BF16), w_ple.astype(BF16), w_ple_gate.astype(BF16),
                     b_ple_gate.reshape(1, -1), ln_g.reshape(1, -1), ln_b.reshape(1, -1), alpha)


def kernel(x, p, w_in, b_gate, conv_qkv, a_log, dt_bias, gdn_norm, conv_dw, conv_dw_bias, conv_ln_g,
           conv_ln_b, forget_bias, w_branch, w_out, w_ple, w_ple_gate, b_ple_gate, ln_g, ln_b):
    batch, seq, dm = x.shape
    depth = w_in.shape[0]
    alpha = (2 * depth) ** 0.25
    x2d = x.reshape(batch * seq, dm)
    for i in range(depth):
        x2d = _layer(x2d, p[i].reshape(batch * seq, -1), batch, seq, alpha, w_in[i], b_gate[i],
                     conv_qkv[i], a_log[i], dt_bias[i], gdn_norm[i], conv_dw[i], conv_dw_bias[i],
                     conv_ln_g[i], conv_ln_b[i], forget_bias[i], w_branch[i], w_out[i], w_ple[i],
                     w_ple_gate[i], b_ple_gate[i], ln_g[i], ln_b[i])
    return x2d.reshape(batch, seq, dm)
```

```python
import functools
import math

import jax
import jax.numpy as jnp
from jax import lax
from jax.experimental import pallas as pl
from jax.experimental.pallas import tpu as pltpu

F32 = jnp.float32
BF16 = jnp.bfloat16

LANE = 128
SUBLANE = 8
EPS = 1e-5
GDN_HEADS = 4
GDN_HEAD_DIM = 128
GDN_CHUNK = 128
GDN_CONV = 4
CONV_K = 31
CONV_HALO = 32
ATT_HEADS = 8
ATT_HEAD_DIM = 64
HEADS_PER_BLOCK = LANE // ATT_HEAD_DIM
N_BRANCH = 4
BRANCH_WIDTH = 512
LOG2E = 1.4426950408889634
LN2 = 0.6931471805599453
EXP2_ZERO_BELOW = -152.0
NORM_SLACK = 1.01
NEG_BIG = -1e30

COL_A_Q, COL_A_K, COL_A_V, COL_A_Z = 0, 4, 8, 12
COL_B_LIN, COL_B_GATE, COL_B_Z = 16, 20, 24
COL_C_Q, COL_C_Z = 28, 32
COL_D_Q, COL_D_Z = 36, 40
COL_SMALL = 44
PROJ_COLS = 45 * LANE
SMALL_A, SMALL_B, SMALL_F = 0, 4, 8
COL_V_D = 0
COL_V_C = 8
V_COLS = (8 + 4) * LANE

VMEM_LIMIT = 48 * 1024 * 1024

TILES = dict(proj_m=256, gdn_t=512, conf_t=512, fcum_t=512, att_q=256, sb_k=128, fox_k=256, epi_m=256)


def _cparams(sem):
    return pltpu.CompilerParams(dimension_semantics=sem, vmem_limit_bytes=VMEM_LIMIT)


def _sigmoid(x):
    return 0.5 * jnp.tanh(0.5 * x) + 0.5


def _silu(x):
    return x * _sigmoid(x)


def _softplus(x):
    return jnp.maximum(x, 0.0) + jnp.log1p(jnp.exp(-jnp.abs(x)))


_NN = (((1,), (0,)), ((), ()))
_NT = (((1,), (1,)), ((), ()))
_TN = (((0,), (0,)), ((), ()))


def _dot1(a, b, dims=_NN):
    return lax.dot_general(a, b, dims, preferred_element_type=F32)


def _dot_split_rhs(a, b):
    b_hi = pltpu.bitcast(pltpu.bitcast(b, jnp.uint32) & jnp.uint32(0xFFFF0000), F32)
    return _dot1(a, b_hi) + _dot1(a, b - b_hi)


def _split3(x):
    hi = x.astype(BF16)
    r1 = x - hi.astype(F32)
    mid = r1.astype(BF16)
    lo = (r1 - mid.astype(F32)).astype(BF16)
    return hi, mid, lo


def _dot_exact_lhs(a01, x):
    a = a01.astype(BF16)
    hi, mid, lo = _split3(x)
    d = lambda p: jnp.dot(a, p, preferred_element_type=F32)
    return d(hi) + d(mid) + d(lo)


def _dot_sum01(x, b01):
    hi = x.astype(BF16)
    lo = (x - hi.astype(F32)).astype(BF16)
    return (jnp.dot(hi, b01, preferred_element_type=F32)
            + jnp.dot(lo, b01, preferred_element_type=F32))


def _proj_kernel(x_ref, wm_ref, wv_ref, bv_ref, wkt_ref, p_ref, v_ref, kt_ref):
    x16 = x_ref[...].astype(BF16)
    p_ref[...] = jnp.dot(x16, wm_ref[...], preferred_element_type=F32)
    v_ref[...] = (jnp.dot(x16, wv_ref[...], preferred_element_type=F32) + bv_ref[...]).astype(BF16)
    kt_ref[...] = lax.dot_general(wkt_ref[...], x16, _NT, preferred_element_type=F32).astype(BF16)


def _proj(x2d, w_main, w_v, b_v, w_kt, batch, seq):
    m, k = x2d.shape
    tm = TILES["proj_m"]
    nt = seq // tm
    once = pl.Buffered(1)
    full = lambda a: pl.BlockSpec(a.shape, lambda i: (0, 0), pipeline_mode=once)
    return pl.pallas_call(
        _proj_kernel,
        grid=(m // tm,),
        in_specs=[pl.BlockSpec((tm, k), lambda i: (i, 0)), full(w_main), full(w_v), full(b_v), full(w_kt)],
        out_specs=[pl.BlockSpec((tm, w_main.shape[1]), lambda i: (i, 0)),
                   pl.BlockSpec((tm, w_v.shape[1]), lambda i: (i, 0)),
                   pl.BlockSpec((None, w_kt.shape[0], tm), lambda i: (i // nt, 0, i % nt))],
        out_shape=[jax.ShapeDtypeStruct((m, w_main.shape[1]), F32),
                   jax.ShapeDtypeStruct((m, w_v.shape[1]), BF16),
                   jax.ShapeDtypeStruct((batch, w_kt.shape[0], seq), BF16)],
        compiler_params=_cparams(("arbitrary",)),
        name="proj",
    )(x2d, w_main, w_v, b_v, w_kt)


def _gdn_kernel(q_ref, k_ref, v_ref, z_ref, qh_ref, kh_ref, vh_ref, sm_ref,
                cw_ref, alog_ref, dtb_ref, gn_ref, o_ref, s_ref, cs_ref, qkv_ref):
    t = pl.program_id(1)
    C = GDN_CHUNK
    W = GDN_HEADS * GDN_HEAD_DIM
    tt = q_ref.shape[0]

    @pl.when(t == 0)
    def _():
        s_ref[...] = jnp.zeros_like(s_ref)

    has_prev = (t > 0).astype(F32)

    def conv_silu(x_ref, h_ref, col0, slot):
        cs_ref[0:SUBLANE, :] = h_ref[...] * has_prev
        cs_ref[SUBLANE:SUBLANE + tt, :] = x_ref[...]
        acc = jnp.zeros((tt, W), F32)
        off = SUBLANE - (GDN_CONV - 1)
        for j in range(GDN_CONV):
            acc = acc + cs_ref[off + j:off + j + tt, :] * cw_ref[j:j + 1, col0:col0 + W]
        qkv_ref[slot] = _silu(acc)
        return qkv_ref[slot]

    q_all = conv_silu(q_ref, qh_ref, 0, 0)
    k_all = conv_silu(k_ref, kh_ref, W, 1)
    v_all = conv_silu(v_ref, vh_ref, 2 * W, 2)

    sm = sm_ref[...]
    g_all = -jnp.exp(alog_ref[...]) * _softplus(sm + dtb_ref[...])
    beta_all = _sigmoid(sm)

    row = lax.broadcasted_iota(jnp.int32, (C, C), 0)
    col = lax.broadcasted_iota(jnp.int32, (C, C), 1)
    incl = row >= col
    strict = row > col
    eye = (row == col).astype(F32)
    tri = incl.astype(F32)

    chunks = range(tt // C)
    units = [(c, h) for c in chunks for h in range(GDN_HEADS)]
    U = range(len(units))
    rs = [slice(c * C, (c + 1) * C) for c, _ in units]
    sls = [slice(h * GDN_HEAD_DIM, (h + 1) * GDN_HEAD_DIM) for _, h in units]
    gc_chunk = [_dot_exact_lhs(tri, g_all[c * C:(c + 1) * C]) for c in chunks]
    gcs = [gc_chunk[c][:, SMALL_A + h:SMALL_A + h + 1] for c, h in units]
    betas = [beta_all[rs[u], SMALL_B + units[u][1]:SMALL_B + units[u][1] + 1] for u in U]
    qs, ks = [], []
    for u in U:
        q = q_all[rs[u], sls[u]]
        k = k_all[rs[u], sls[u]]
        qs.append(q * (lax.rsqrt(jnp.sum(q * q, axis=-1, keepdims=True) + 1e-6) * (GDN_HEAD_DIM ** -0.5)))
        ks.append(k * lax.rsqrt(jnp.sum(k * k, axis=-1, keepdims=True) + 1e-6))
    gc_rows = [jnp.broadcast_to(gcs[u], (C, C)).T for u in U]
    decays = [jnp.where(incl, jnp.exp(jnp.where(incl, gcs[u] - gc_rows[u], 0.0)), 0.0) for u in U]
    kbs = [ks[u] * betas[u] for u in U]
    e_gcs = [jnp.exp(gcs[u]) for u in U]
    n_pows = [-jnp.where(strict, _dot1(kbs[u], ks[u], _NT) * decays[u], 0.0) for u in U]
    invs = [eye + n_pows[u] for u in U]
    for _ in range(int(math.log2(C)) - 1):
        n_pows = [_dot1(n_pows[u], n_pows[u]) for u in U]
        invs = [invs[u] + _dot1(invs[u], n_pows[u]) for u in U]
    sols = [_dot_split_rhs(invs[u], jnp.concatenate([v_all[rs[u], sls[u]] * betas[u], kbs[u] * e_gcs[u]], axis=1))
            for u in U]
    a_qks = [_dot1(qs[u], ks[u], _NT) * decays[u] for u in U]
    for u in U:
        h = units[u][1]
        usol = sols[u][:, :GDN_HEAD_DIM]
        w = sols[u][:, GDN_HEAD_DIM:]
        gc_last = gcs[u][C - 1:C, :]
        k_dec = ks[u] * jnp.exp(gc_last - gcs[u])
        s = s_ref[h]
        v_new = usol - _dot1(w, s)
        o = _dot1(qs[u] * e_gcs[u], s) + _dot1(a_qks[u], v_new)
        s_ref[h] = s * jnp.exp(gc_last) + _dot1(k_dec, v_new, _TN)
        o = o * lax.rsqrt(jnp.mean(o * o, axis=-1, keepdims=True) + EPS) * gn_ref[...]
        o_ref[rs[u], sls[u]] = o * _silu(z_ref[rs[u], sls[u]])


def _gdn(proj, conv_qkv, a_log_row, dt_row, gn_row, batch, seq):
    tt = TILES["gdn_t"]
    assert tt % GDN_CHUNK == 0
    W = GDN_HEADS * GDN_HEAD_DIM
    nt = seq // tt
    m = batch * seq
    wb = W // LANE

    def main(colblk):
        return pl.BlockSpec((tt, W), lambda b, t: (b * nt + t, colblk // wb))

    def halo(colblk):
        return pl.BlockSpec((SUBLANE, W),
                            lambda b, t: (jnp.maximum((b * nt + t) * (tt // SUBLANE) - 1, 0), colblk // wb))

    full = lambda shp: pl.BlockSpec(shp, lambda b, t: (0,) * len(shp))
    return pl.pallas_call(
        _gdn_kernel,
        grid=(batch, nt),
        in_specs=[main(COL_A_Q), main(COL_A_K), main(COL_A_V), main(COL_A_Z),
                  halo(COL_A_Q), halo(COL_A_K), halo(COL_A_V),
                  pl.BlockSpec((tt, LANE), lambda b, t: (b * nt + t, COL_SMALL)),
                  full((GDN_CONV, 3 * W)), full((1, LANE)), full((1, LANE)), full((1, GDN_HEAD_DIM))],
        out_specs=pl.BlockSpec((tt, W), lambda b, t: (b * nt + t, 0)),
        out_shape=jax.ShapeDtypeStruct((m, W), F32),
        scratch_shapes=[pltpu.VMEM((GDN_HEADS, GDN_HEAD_DIM, GDN_HEAD_DIM), F32),
                        pltpu.VMEM((tt + SUBLANE, W), F32),
                        pltpu.VMEM((3, tt, W), F32)],
        compiler_params=_cparams(("arbitrary", "arbitrary")),
        name="gdn",
    )(proj, proj, proj, proj, proj, proj, proj, proj, conv_qkv, a_log_row, dt_row, gn_row)


def _conf_kernel(lin_ref, gate_ref, z_ref, linh_ref, gateh_ref, cw_ref, cb_ref, g_ref, b_ref,
                 o_ref, cs_ref, sh_ref):
    t = pl.program_id(1)
    tt = lin_ref.shape[0]
    H = CONV_HALO
    has_prev = (t > 0).astype(F32)
    cs_ref[0:H, :] = linh_ref[...] * _sigmoid(gateh_ref[...]) * has_prev
    cs_ref[H:H + tt, :] = lin_ref[...] * _sigmoid(gate_ref[...])
    span = tt + H - SUBLANE
    for r in range(1, SUBLANE):
        sh_ref[r - 1, 0:span, :] = cs_ref[r:r + span, :]
    acc = jnp.zeros(o_ref.shape, F32) + cb_ref[...]
    off = H - (CONV_K - 1)
    for j in range(CONV_K):
        a, r = divmod(off + j, SUBLANE)
        tap = cs_ref[a * SUBLANE:a * SUBLANE + tt, :] if r == 0 else sh_ref[r - 1, a * SUBLANE:a * SUBLANE + tt, :]
        acc = acc + tap * cw_ref[j:j + 1, :]
    mu = jnp.mean(acc, axis=-1, keepdims=True)
    xc = acc - mu
    var = jnp.mean(xc * xc, axis=-1, keepdims=True)
    hb = xc * lax.rsqrt(var + EPS) * g_ref[...] + b_ref[...]
    o_ref[...] = _silu(hb) * _silu(z_ref[...])


def _conf(proj, conv_dw, conv_dw_bias, ln_g, ln_b, batch, seq):
    Wc = BRANCH_WIDTH
    tt = TILES["conf_t"]
    nt = seq // tt
    m = batch * seq
    wb = Wc // LANE
    H = CONV_HALO

    def main(colblk):
        return pl.BlockSpec((tt, Wc), lambda b, t: (b * nt + t, colblk // wb))

    def halo(colblk):
        return pl.BlockSpec((H, Wc), lambda b, t: (jnp.maximum((b * nt + t) * (tt // H) - 1, 0), colblk // wb))

    full = lambda shp: pl.BlockSpec(shp, lambda b, t: (0,) * len(shp))
    return pl.pallas_call(
        _conf_kernel,
        grid=(batch, nt),
        in_specs=[main(COL_B_LIN), main(COL_B_GATE), main(COL_B_Z), halo(COL_B_LIN), halo(COL_B_GATE),
                  full((CONV_K, Wc)), full((1, Wc)), full((1, Wc)), full((1, Wc))],
        out_specs=pl.BlockSpec((tt, Wc), lambda b, t: (b * nt + t, 0)),
        out_shape=jax.ShapeDtypeStruct((m, Wc), F32),
        scratch_shapes=[pltpu.VMEM((tt + H, Wc), F32),
                        pltpu.VMEM((SUBLANE - 1, tt + H - SUBLANE, Wc), F32)],
        compiler_params=_cparams(("arbitrary", "arbitrary")),
        name="conformer_conv",
    )(proj, proj, proj, proj, proj, conv_dw, conv_dw_bias, ln_g, ln_b)


def _merge_head_lanes(accs):
    rows = accs[0].shape[0]
    lane = lax.broadcasted_iota(jnp.int32, (rows, LANE), 1)
    pairs = [jnp.where(lane < ATT_HEAD_DIM, accs[h], accs[h + 1]) for h in range(0, len(accs), HEADS_PER_BLOCK)]
    return jnp.concatenate(pairs, axis=1)


def _sb_block(qh, kt_ref, v_ref, k0, tk, carry_ref, acc_ref, later_ext, r0, mfac):
    d = ATT_HEAD_DIM
    heads = range(len(qh))
    s2 = [jnp.dot(qh[h][r0:], kt_ref[h * d:(h + 1) * d, pl.ds(k0, tk)], preferred_element_type=F32)
          for h in heads]
    lb, lr = [], []
    for h in heads:
        lp = jnp.log(1.0 + jnp.exp2(-jnp.abs(s2[h]))) * LOG2E
        b = jnp.minimum(s2[h], 0.0) - lp
        r = b - s2[h]
        lb.append(b)
        lr.append(r if mfac is None else r * mfac)
    bx = [_dot_sum01(lr[h], later_ext[:, :tk]) for h in heads]
    rsum = [jnp.sum(lr[h], axis=-1, keepdims=True) for h in heads]
    att = []
    for h in heads:
        a = jnp.exp2(lb[h] + bx[h] + carry_ref[h, r0:, :])
        att.append((a if mfac is None else a * mfac).astype(BF16))
    for h in heads:
        blk = h // HEADS_PER_BLOCK
        acc_ref[h, r0:, :] += jnp.dot(att[h], v_ref[pl.ds(k0, tk), blk * LANE:(blk + 1) * LANE],
                                      preferred_element_type=F32)
        carry_ref[h, r0:, :] += rsum[h]


def _sb_kernel(q_ref, kt_ref, v_ref, z_ref, o_ref, carry_ref, acc_ref, *, tq, tk):
    qi = pl.program_id(1)
    d = ATT_HEAD_DIM
    nh = ATT_HEADS
    nsub = tq // tk
    assert tk == LANE and tq % tk == 0
    jr = lax.broadcasted_iota(jnp.int32, (tk, tk + LANE), 0)
    sc = lax.broadcasted_iota(jnp.int32, (tk, tk + LANE), 1)
    later_ext = jnp.logical_or(jr > sc, sc >= tk).astype(BF16)

    q2 = (q_ref[...] * (LOG2E * d ** -0.5)).astype(BF16)
    qh = [q2[:, h * d:(h + 1) * d] for h in range(nh)]
    carry_ref[...] = jnp.zeros_like(carry_ref)
    acc_ref[...] = jnp.zeros_like(acc_ref)
    block = functools.partial(_sb_block, qh, kt_ref, v_ref)

    for c in range(nsub - 1, -1, -1):
        r0 = c * tk
        row = lax.broadcasted_iota(jnp.int32, (tq - r0, tk), 0)
        kidx = lax.broadcasted_iota(jnp.int32, (tq - r0, tk), 1)
        block(pl.multiple_of(qi * tq + r0, tk), tk, carry_ref, acc_ref, later_ext, r0, (kidx < row).astype(F32))

    nold = qi * nsub

    def live():
        mx = carry_ref[0]
        for h in range(1, nh):
            mx = jnp.maximum(mx, carry_ref[h])
        return (jnp.max(mx) > EXP2_ZERO_BELOW).astype(jnp.int32)

    def cond(st):
        return jnp.logical_and(st[0] < nold, st[1] > 0)

    def body(st):
        block(pl.multiple_of((nold - 1 - st[0]) * tk, tk), tk, carry_ref, acc_ref, later_ext, 0, None)
        return st[0] + 1, live()

    lax.while_loop(cond, body, (jnp.int32(0), live()))
    o_ref[...] = _merge_head_lanes([acc_ref[h] for h in range(nh)]) * _silu(z_ref[...])


def _sb_attention(proj, kt_all, v_all, batch, seq):
    tq, tk = TILES["att_q"], TILES["sb_k"]
    m = batch * seq
    nq = seq // tq
    W = ATT_HEADS * ATT_HEAD_DIM
    wb = W // LANE
    once = pl.Buffered(1)
    return pl.pallas_call(
        functools.partial(_sb_kernel, tq=tq, tk=tk),
        grid=(batch, nq),
        in_specs=[pl.BlockSpec((tq, W), lambda b, i: (b * nq + i, COL_C_Q // wb)),
                  pl.BlockSpec((None, W, seq), lambda b, i: (b, 0, 0), pipeline_mode=once),
                  pl.BlockSpec((seq, W), lambda b, i: (b, COL_V_C // wb), pipeline_mode=once),
                  pl.BlockSpec((tq, W), lambda b, i: (b * nq + i, COL_C_Z // wb))],
        out_specs=pl.BlockSpec((tq, W), lambda b, i: (b * nq + i, 0)),
        out_shape=jax.ShapeDtypeStruct((m, W), F32),
        scratch_shapes=[pltpu.VMEM((ATT_HEADS, tq, LANE), F32), pltpu.VMEM((ATT_HEADS, tq, LANE), F32)],
        compiler_params=_cparams(("arbitrary", "arbitrary")),
        name="stick_breaking_attention",
    )(proj, kt_all, v_all, proj)


def _fcum_kernel(sm_ref, fb_ref, ft_ref, pm_ref, carry_ref, pmin_ref, *, blk):
    t = pl.program_id(1)
    tt = sm_ref.shape[0]

    @pl.when(t == 0)
    def _():
        carry_ref[...] = jnp.zeros_like(carry_ref)
        pmin_ref[...] = jnp.zeros_like(pmin_ref)

    x = sm_ref[...] + fb_ref[...]
    log_f = jnp.minimum(x, 0.0) - jnp.log1p(jnp.exp(-jnp.abs(x)))
    row = lax.broadcasted_iota(jnp.int32, (tt, tt), 0)
    col = lax.broadcasted_iota(jnp.int32, (tt, tt), 1)
    f = _dot_exact_lhs((row >= col).astype(F32), log_f) + carry_ref[...]
    carry_ref[...] = f[tt - 1:tt, :]
    ft_ref[...] = f.T[SMALL_F:SMALL_F + ATT_HEADS, :]
    rid = lax.broadcasted_iota(jnp.int32, (SUBLANE, LANE), 0)
    pm = pmin_ref[...]
    out = jnp.zeros((SUBLANE, LANE), F32)
    for r in range(tt // blk):
        pm = jnp.minimum(pm, jnp.min(f[r * blk:(r + 1) * blk], axis=0, keepdims=True))
        out = jnp.where(rid == r, pm, out)
    pmin_ref[...] = pm
    pm_ref[...] = out


def _fcum(proj, fb_row, batch, seq):
    tt, blk = TILES["fcum_t"], TILES["fox_k"]
    nt = seq // tt
    per = tt // blk
    ft, pm = pl.pallas_call(
        functools.partial(_fcum_kernel, blk=blk),
        grid=(batch, nt),
        in_specs=[pl.BlockSpec((tt, LANE), lambda b, t: (b * nt + t, COL_SMALL)),
                  pl.BlockSpec((1, LANE), lambda b, t: (0, 0))],
        out_specs=[pl.BlockSpec((None, ATT_HEADS, tt), lambda b, t: (b, 0, t)),
                   pl.BlockSpec((None, SUBLANE, LANE), lambda b, t: (b * nt + t, 0, 0))],
        out_shape=[jax.ShapeDtypeStruct((batch, ATT_HEADS, seq), F32),
                   jax.ShapeDtypeStruct((batch * nt, SUBLANE, LANE), F32)],
        scratch_shapes=[pltpu.VMEM((1, LANE), F32), pltpu.VMEM((1, LANE), F32)],
        compiler_params=_cparams(("arbitrary", "arbitrary")),
        name="forget_cumsum",
    )(proj, fb_row)
    pm = pm[:, :per, SMALL_F:SMALL_F + ATT_HEADS].reshape(batch, nt * per, ATT_HEADS)
    return ft, jnp.transpose(pm, (0, 2, 1)).reshape(-1)


def _fox_block(qh, kt_ref, v_ref, fk_ref, cref, k0, tk, m_ref, acc_ref, mode, heads):
    d = ATT_HEAD_DIM
    pv = lambda pr, h: jnp.dot(pr, v_ref[pl.ds(k0, tk), h * LANE:(h + 1) * LANE], preferred_element_type=F32)
    z = {}
    for h in heads:
        fk2 = (fk_ref[h:h + 1, pl.ds(k0, tk)] - cref[h]) * LOG2E
        z[h] = jnp.dot(qh[h], kt_ref[h * d:(h + 1) * d, pl.ds(k0, tk)], preferred_element_type=F32) - fk2
    if mode == "plain":
        pr = {h: jnp.exp2(z[h] - m_ref[h]).astype(BF16) for h in heads}
        for h in heads:
            acc_ref[h] += pv(pr[h], h)
        return
    if mode == "first":
        shp = (qh[0].shape[0], tk)
        keep = lax.broadcasted_iota(jnp.int32, shp, 1) <= lax.broadcasted_iota(jnp.int32, shp, 0)
        z = {h: jnp.where(keep, z[h], NEG_BIG) for h in heads}
        m_new = {h: jnp.max(z[h], axis=-1, keepdims=True) for h in heads}
    else:
        m_old = {h: m_ref[h] for h in heads}
        m_new = {h: jnp.maximum(m_old[h], jnp.max(z[h], axis=-1, keepdims=True)) for h in heads}
    pr = {h: jnp.exp2(z[h] - m_new[h]).astype(BF16) for h in heads}
    for h in heads:
        if mode == "first":
            acc_ref[h] = pv(pr[h], h)
        else:
            acc_ref[h] = jnp.exp2(m_old[h] - m_new[h]) * acc_ref[h] + pv(pr[h], h)
        m_ref[h] = m_new[h]


def _fox_kernel(pm_ref, q_ref, kt_ref, v_ref, z_ref, fk_ref, o_ref, kmax_ref, m_ref, acc_ref, *, tq, tk, seq):
    b = pl.program_id(0)
    qi = pl.program_id(1)
    d = ATT_HEAD_DIM
    nh = ATT_HEADS
    nblk = seq // tk
    assert tq == tk

    @pl.when(qi == 0)
    def _():
        step = 2048 if seq % 2048 == 0 else tk
        for h in range(nh):
            best = jnp.zeros((1, step), F32)
            for c0 in range(0, seq, step):
                kf = kt_ref[h * d:(h + 1) * d, c0:c0 + step].astype(F32)
                best = jnp.maximum(best, jnp.sum(kf * kf, axis=0, keepdims=True))
            kmax_ref[h] = jnp.broadcast_to(jnp.sqrt(jnp.max(best, axis=-1, keepdims=True)), (SUBLANE, LANE))

    q2 = (q_ref[...] * (LOG2E * d ** -0.5)).astype(BF16)

    qh, ub, base, cref = [], [], [], []
    for h in range(nh):
        qh.append(q2[:, h * d:(h + 1) * d])
        qf = qh[h].astype(F32)
        qn = jnp.sqrt(jnp.sum(qf * qf, axis=-1, keepdims=True))
        ub.append(qn * kmax_ref[h, 0:1, 0:1] * NORM_SLACK)
        base.append((b * nh + h) * nblk)
        cref.append(jnp.where(qi > 0, pm_ref[base[h] + jnp.maximum(qi - 1, 0)], 0.0))

    block = functools.partial(_fox_block, qh, kt_ref, v_ref, fk_ref, cref)
    all_heads = list(range(nh))
    block(pl.multiple_of(qi * tk, tk), tk, m_ref, acc_ref, "first", all_heads)

    def slack(heads, old=None):
        return [jnp.max(ub[h] - m_ref[h]) if h in heads else old[h] for h in range(nh)]

    def excess(sl, j, heads):
        jj = jnp.maximum(j, 0)
        w = None
        for h in heads:
            t = sl[h] + (cref[h] - pm_ref[base[h] + jj]) * LOG2E
            w = t if w is None else jnp.maximum(w, t)
        return w

    def sweep(st, heads, watch):
        def cond(st):
            return jnp.logical_and(st[0] < qi, excess(st[1:], qi - 1 - st[0], watch) > EXP2_ZERO_BELOW)

        def body(st):
            i = st[0]
            j = qi - 1 - i
            k0 = pl.multiple_of(j * tk, tk)
            sl = list(st[1:])

            def plain():
                block(k0, tk, m_ref, acc_ref, "plain", heads)
                return tuple(sl)

            def online():
                block(k0, tk, m_ref, acc_ref, "online", heads)
                return tuple(slack(heads, sl))

            return (i + 1, *lax.cond(excess(sl, j, heads) <= 0.0, plain, online))

        return lax.while_loop(cond, body, st)

    st = (jnp.int32(0), *slack(all_heads))
    heads = all_heads
    while heads:
        half = max(len(heads) // 2, 1) if len(heads) > HEADS_PER_BLOCK else len(heads)
        st = sweep(st, heads, heads[:half])
        heads = heads[half:]
    outs = [acc_ref[h] / pltpu.roll(acc_ref[h], d, 1) for h in range(nh)]
    o_ref[...] = _merge_head_lanes(outs) * _silu(z_ref[...])


def _fox_attention(proj, kt_all, v_all, f_t, pm_tab, batch, seq):
    tq, tk = TILES["att_q"], TILES["fox_k"]
    m = batch * seq
    nq = seq // tq
    W = ATT_HEADS * ATT_HEAD_DIM
    wb = W // LANE
    assert COL_V_D == 0
    once = pl.Buffered(1)
    return pl.pallas_call(
        functools.partial(_fox_kernel, tq=tq, tk=tk, seq=seq),
        grid=(batch, nq),
        in_specs=[pl.BlockSpec(memory_space=pltpu.SMEM),
                  pl.BlockSpec((tq, W), lambda b, i: (b * nq + i, COL_D_Q // wb)),
                  pl.BlockSpec((None, W, seq), lambda b, i: (b, 1, 0), pipeline_mode=once),
                  pl.BlockSpec((seq, ATT_HEADS * LANE), lambda b, i: (b, 0), pipeline_mode=once),
                  pl.BlockSpec((tq, W), lambda b, i: (b * nq + i, COL_D_Z // wb)),
                  pl.BlockSpec((None, ATT_HEADS, seq), lambda b, i: (b, 0, 0), pipeline_mode=once)],
        out_specs=pl.BlockSpec((tq, W), lambda b, i: (b * nq + i, 0)),
        out_shape=jax.ShapeDtypeStruct((m, W), F32),
        scratch_shapes=[pltpu.VMEM((ATT_HEADS, SUBLANE, LANE), F32),
                        pltpu.VMEM((ATT_HEADS, tq, 1), F32),
                        pltpu.VMEM((ATT_HEADS, tq, LANE), F32)],
        compiler_params=_cparams(("arbitrary", "arbitrary")),
        name="forgetting_attention",
    )(pm_tab, proj, kt_all, v_all, proj, f_t)


def _epi_kernel(x_ref, ya_ref, yb_ref, yc_ref, yd_ref, p_ref, wg_ref, bg_ref, wb_ref, wo_ref,
                wp_ref, wpg_ref, bpg_ref, lg_ref, lb_ref, o_ref, *, alpha):
    dm = x_ref.shape[1]
    x16 = x_ref[...].astype(BF16)
    merged = jnp.zeros(x_ref.shape, F32)
    for br, y_ref in enumerate((ya_ref, yb_ref, yc_ref, yd_ref)):
        cs = slice(br * dm, (br + 1) * dm)
        gate = _sigmoid(jnp.dot(x16, wg_ref[:, cs], preferred_element_type=F32) + bg_ref[:, cs])
        merged = merged + gate * jnp.dot(y_ref[...].astype(BF16), wb_ref[br], preferred_element_type=F32)
    mix = jnp.dot(merged.astype(BF16), wo_ref[...], preferred_element_type=F32)
    r = alpha * x_ref[...] + mix
    pg = _sigmoid(jnp.dot(r.astype(BF16), wpg_ref[...], preferred_element_type=F32) + bpg_ref[...])
    r = r + pg * jnp.dot(p_ref[...].astype(BF16), wp_ref[...], preferred_element_type=F32)
    mu = jnp.mean(r, axis=-1, keepdims=True)
    rc = r - mu
    var = jnp.mean(rc * rc, axis=-1, keepdims=True)
    o_ref[...] = rc * lax.rsqrt(var + EPS) * lg_ref[...] + lb_ref[...]


def _epilogue(x2d, ya, yb, yc, yd, p2d, w_gates, b_gate, w_branch, w_out, w_ple, w_ple_gate, b_ple_gate,
              ln_g, ln_b, alpha):
    m, dm = x2d.shape
    tm = TILES["epi_m"]
    row = lambda w: pl.BlockSpec((tm, w), lambda i: (i, 0))
    once = pl.Buffered(1)
    full = lambda shp: pl.BlockSpec(shp, lambda i: (0,) * len(shp), pipeline_mode=once)
    return pl.pallas_call(
        functools.partial(_epi_kernel, alpha=alpha),
        grid=(m // tm,),
        in_specs=[row(dm),
                  row(BRANCH_WIDTH), row(BRANCH_WIDTH), row(BRANCH_WIDTH), row(BRANCH_WIDTH),
                  row(p2d.shape[1]),
                  full(w_gates.shape), full((1, N_BRANCH * dm)), full(w_branch.shape), full(w_out.shape),
                  full(w_ple.shape), full(w_ple_gate.shape), full((1, dm)), full((1, dm)), full((1, dm))],
        out_specs=row(dm),
        out_shape=jax.ShapeDtypeStruct((m, dm), F32),
        compiler_params=_cparams(("arbitrary",)),
        name="epilogue",
    )(x2d, ya, yb, yc, yd, p2d, w_gates, b_gate, w_branch, w_out, w_ple, w_ple_gate, b_ple_gate, ln_g, ln_b)


def _split_w_in(w_in, order_d):
    gw, cw, aw = GDN_HEADS * GDN_HEAD_DIM, BRANCH_WIDTH, ATT_HEADS * ATT_HEAD_DIM
    d = ATT_HEAD_DIM
    dm = w_in.shape[0]
    sizes = ([gw] * 4 + [GDN_HEADS, GDN_HEADS] + [2 * cw, cw] + [aw] * 4 + [aw] * 4 + [ATT_HEADS]
             + [N_BRANCH * dm])
    assert sum(sizes) == w_in.shape[1]
    pts = [sum(sizes[:i + 1]) for i in range(len(sizes) - 1)]
    (qa, ka, va, za, aa, ba, glu, zb, qc, kc, vc, zc, qd, kd, vd, zd, fd, gates) = jnp.split(w_in, pts, axis=1)
    by_head = lambda w: w.reshape(dm, ATT_HEADS, d)[:, order_d, :].reshape(dm, aw)
    qd, kd, vd, zd, fd = by_head(qd), by_head(kd), by_head(vd), by_head(zd), fd[:, order_d]
    small = jnp.concatenate([aa, ba, fd], axis=1)
    small = jnp.pad(small, ((0, 0), (0, LANE - small.shape[1])))
    w_main = jnp.concatenate([qa, ka, va, za, glu, zb, qc, zc, qd, zd, small], axis=1)
    assert w_main.shape[1] == PROJ_COLS
    w_kt = jnp.concatenate([kc, kd], axis=1).T
    zeros = jnp.zeros((dm, d), w_in.dtype)
    vd_blocks, bias_blocks = [], []
    for h in range(ATT_HEADS):
        vh = vd[:, h * d:(h + 1) * d]
        pair = [vh, zeros] if h % 2 == 0 else [zeros, vh]
        ones = [jnp.zeros((d,), F32), jnp.ones((d,), F32)]
        vd_blocks += pair
        bias_blocks += ones if h % 2 == 0 else ones[::-1]
    w_v = jnp.concatenate(vd_blocks + [vc], axis=1)
    b_v = jnp.concatenate(bias_blocks + [jnp.zeros((aw,), F32)]).reshape(1, -1)
    assert w_v.shape[1] == V_COLS
    return w_main.astype(BF16), w_kt.astype(BF16), w_v.astype(BF16), b_v, gates.astype(BF16)


def _lane_row(vec, offset):
    return jnp.zeros((1, LANE), F32).at[0, offset:offset + vec.shape[0]].set(vec.astype(F32))


def _layer(x2d, p2d, batch, seq, alpha, w_in, b_gate, conv_qkv, a_log, dt_bias, gdn_norm, conv_dw,
           conv_dw_bias, conv_ln_g, conv_ln_b, forget_bias, w_branch, w_out, w_ple, w_ple_gate,
           b_ple_gate, ln_g, ln_b):
    order_d = jnp.argsort(forget_bias)
    forget_bias = forget_bias[order_d]
    wd = w_branch[3].reshape(ATT_HEADS, ATT_HEAD_DIM, -1)[order_d].reshape(w_branch.shape[1:])
    w_branch = jnp.concatenate([w_branch[:3], wd[None]], axis=0)
    w_main, w_kt, w_v, b_v, w_gates = _split_w_in(w_in, order_d)
    proj, v_all, kt_all = _proj(x2d, w_main, w_v, b_v, w_kt, batch, seq)

    ya = _gdn(proj, conv_qkv, _lane_row(a_log, SMALL_A), _lane_row(dt_bias, SMALL_A),
              gdn_norm.reshape(1, -1), batch, seq)
    yb = _conf(proj, conv_dw, conv_dw_bias.reshape(1, -1), conv_ln_g.reshape(1, -1),
               conv_ln_b.reshape(1, -1), batch, seq)
    yc = _sb_attention(proj, kt_all, v_all, batch, seq)
    f_t, pm_tab = _fcum(proj, _lane_row(forget_bias, SMALL_F), batch, seq)
    yd = _fox_attention(proj, kt_all, v_all, f_t, pm_tab, batch, seq)
    return _epilogue(x2d, ya, yb, yc, yd, p2d, w_gates, b_gate.reshape(1, -1), w_branch.astype(BF16),
                     w_out.astype(BF16), w_ple.astype(BF16), w_ple_gate.astype(BF16),
                     b_ple_gate.reshape(1, -1), ln_g.reshape(1, -1), ln_b.reshape(1, -1), alpha)


def kernel(x, p, w_in, b_gate, conv_qkv, a_log, dt_bias, gdn_norm, conv_dw, conv_dw_bias, conv_ln_g,
           conv_ln_b, forget_bias, w_branch, w_out, w_ple, w_ple_gate, b_ple_gate, ln_g, ln_b):
    batch, seq, dm = x.shape
    depth = w_in.shape[0]
    alpha = (2 * depth) ** 0.25
    x2d = x.reshape(batch * seq, dm)
    for i in range(depth):
        x2d = _layer(x2d, p[i].reshape(batch * seq, -1), batch, seq, alpha, w_in[i], b_gate[i],
                     conv_qkv[i], a_log[i], dt_bias[i], gdn_norm[i], conv_dw[i], conv_dw_bias[i],
                     conv_ln_g[i], conv_ln_b[i], forget_bias[i], w_branch[i], w_out[i], w_ple[i],
                     w_ple_gate[i], b_ple_gate[i], ln_g[i], ln_b[i])
    return x2d.reshape(batch, seq, dm)
```

```python
import functools
import math

import jax
import jax.numpy as jnp
from jax import lax
from jax.experimental import pallas as pl
from jax.experimental.pallas import tpu as pltpu

F32 = jnp.float32
BF16 = jnp.bfloat16

LANE = 128
SUBLANE = 8
EPS = 1e-5
GDN_HEADS = 4
GDN_HEAD_DIM = 128
GDN_CHUNK = 128
GDN_CONV = 4
CONV_K = 31
CONV_HALO = 32
ATT_HEADS = 8
ATT_HEAD_DIM = 64
HEADS_PER_BLOCK = LANE // ATT_HEAD_DIM
N_BRANCH = 4
BRANCH_WIDTH = 512
LOG2E = 1.4426950408889634
LN2 = 0.6931471805599453
EXP2_ZERO_BELOW = -152.0
NORM_SLACK = 1.01
NEG_BIG = -1e30

COL_A_Q, COL_A_K, COL_A_V, COL_A_Z = 0, 4, 8, 12
COL_B_LIN, COL_B_GATE, COL_B_Z = 16, 20, 24
COL_C_Q, COL_C_Z = 28, 32
COL_D_Q, COL_D_Z = 36, 40
COL_SMALL = 44
PROJ_COLS = 45 * LANE
SMALL_A, SMALL_B, SMALL_F = 0, 4, 8
COL_V_D = 0
COL_V_C = 8
V_COLS = (8 + 4) * LANE

VMEM_LIMIT = 48 * 1024 * 1024

TILES = dict(proj_m=256, gdn_t=512, conf_t=512, fcum_t=512, att_q=256, sb_k=128, fox_k=256, epi_m=512)


def _cparams(sem):
    return pltpu.CompilerParams(dimension_semantics=sem, vmem_limit_bytes=VMEM_LIMIT)


def _sigmoid(x):
    return 0.5 * jnp.tanh(0.5 * x) + 0.5


def _silu(x):
    return x * _sigmoid(x)


def _softplus(x):
    return jnp.maximum(x, 0.0) + jnp.log1p(jnp.exp(-jnp.abs(x)))


_NN = (((1,), (0,)), ((), ()))
_NT = (((1,), (1,)), ((), ()))
_TN = (((0,), (0,)), ((), ()))


def _dot1(a, b, dims=_NN):
    return lax.dot_general(a, b, dims, preferred_element_type=F32)


def _dot_split_rhs(a, b):
    b_hi = pltpu.bitcast(pltpu.bitcast(b, jnp.uint32) & jnp.uint32(0xFFFF0000), F32)
    return _dot1(a, b_hi) + _dot1(a, b - b_hi)


def _split3(x):
    hi = x.astype(BF16)
    r1 = x - hi.astype(F32)
    mid = r1.astype(BF16)
    lo = (r1 - mid.astype(F32)).astype(BF16)
    return hi, mid, lo


def _dot_exact_lhs(a01, x):
    a = a01.astype(BF16)
    hi, mid, lo = _split3(x)
    d = lambda p: jnp.dot(a, p, preferred_element_type=F32)
    return d(hi) + d(mid) + d(lo)


def _dot_sum01(x, b01):
    hi = x.astype(BF16)
    lo = (x - hi.astype(F32)).astype(BF16)
    return (jnp.dot(hi, b01, preferred_element_type=F32)
            + jnp.dot(lo, b01, preferred_element_type=F32))


def _proj_kernel(x_ref, wm_ref, wv_ref, bv_ref, wkt_ref, p_ref, v_ref, kt_ref):
    x16 = x_ref[...].astype(BF16)
    p_ref[...] = jnp.dot(x16, wm_ref[...], preferred_element_type=F32)
    v_ref[...] = (jnp.dot(x16, wv_ref[...], preferred_element_type=F32) + bv_ref[...]).astype(BF16)
    kt_ref[...] = lax.dot_general(wkt_ref[...], x16, _NT, preferred_element_type=F32).astype(BF16)


def _proj(x2d, w_main, w_v, b_v, w_kt, batch, seq):
    m, k = x2d.shape
    tm = TILES["proj_m"]
    nt = seq // tm
    once = pl.Buffered(1)
    full = lambda a: pl.BlockSpec(a.shape, lambda i: (0, 0), pipeline_mode=once)
    return pl.pallas_call(
        _proj_kernel,
        grid=(m // tm,),
        in_specs=[pl.BlockSpec((tm, k), lambda i: (i, 0)), full(w_main), full(w_v), full(b_v), full(w_kt)],
        out_specs=[pl.BlockSpec((tm, w_main.shape[1]), lambda i: (i, 0)),
                   pl.BlockSpec((tm, w_v.shape[1]), lambda i: (i, 0)),
                   pl.BlockSpec((None, w_kt.shape[0], tm), lambda i: (i // nt, 0, i % nt))],
        out_shape=[jax.ShapeDtypeStruct((m, w_main.shape[1]), F32),
                   jax.ShapeDtypeStruct((m, w_v.shape[1]), BF16),
                   jax.ShapeDtypeStruct((batch, w_kt.shape[0], seq), BF16)],
        compiler_params=_cparams(("arbitrary",)),
        name="proj",
    )(x2d, w_main, w_v, b_v, w_kt)


def _gdn_kernel(q_ref, k_ref, v_ref, z_ref, qh_ref, kh_ref, vh_ref, sm_ref,
                cw_ref, alog_ref, dtb_ref, gn_ref, o_ref, s_ref, cs_ref):
    t = pl.program_id(1)
    C = GDN_CHUNK
    W = GDN_HEADS * GDN_HEAD_DIM
    tt = q_ref.shape[0]

    @pl.when(t == 0)
    def _():
        s_ref[...] = jnp.zeros_like(s_ref)

    has_prev = (t > 0).astype(F32)

    def conv_silu(x_ref, h_ref, col0):
        cs_ref[0:SUBLANE, :] = h_ref[...] * has_prev
        cs_ref[SUBLANE:SUBLANE + tt, :] = x_ref[...]
        xs = cs_ref[...]
        acc = jnp.zeros((tt, W), F32)
        for j in range(GDN_CONV):
            back = GDN_CONV - 1 - j
            tap = xs if back == 0 else pltpu.roll(xs, back, 0)
            acc = acc + tap[SUBLANE:SUBLANE + tt] * cw_ref[j:j + 1, col0:col0 + W]
        return _silu(acc)

    q_all = conv_silu(q_ref, qh_ref, 0)
    k_all = conv_silu(k_ref, kh_ref, W)
    v_all = conv_silu(v_ref, vh_ref, 2 * W)

    sm = sm_ref[...]
    g_all = -jnp.exp(alog_ref[...]) * _softplus(sm + dtb_ref[...])
    beta_all = _sigmoid(sm)

    row = lax.broadcasted_iota(jnp.int32, (C, C), 0)
    col = lax.broadcasted_iota(jnp.int32, (C, C), 1)
    incl = row >= col
    strict = row > col
    eye = (row == col).astype(F32)
    tri = incl.astype(F32)

    chunks = range(tt // C)
    units = [(c, h) for c in chunks for h in range(GDN_HEADS)]
    U = range(len(units))
    rs = [slice(c * C, (c + 1) * C) for c, _ in units]
    sls = [slice(h * GDN_HEAD_DIM, (h + 1) * GDN_HEAD_DIM) for _, h in units]
    gc_chunk = [_dot_exact_lhs(tri, g_all[c * C:(c + 1) * C]) for c in chunks]
    gcs = [gc_chunk[c][:, SMALL_A + h:SMALL_A + h + 1] for c, h in units]
    betas = [beta_all[rs[u], SMALL_B + units[u][1]:SMALL_B + units[u][1] + 1] for u in U]
    qs, ks = [], []
    for u in U:
        q = q_all[rs[u], sls[u]]
        k = k_all[rs[u], sls[u]]
        qs.append(q * (lax.rsqrt(jnp.sum(q * q, axis=-1, keepdims=True) + 1e-6) * (GDN_HEAD_DIM ** -0.5)))
        ks.append(k * lax.rsqrt(jnp.sum(k * k, axis=-1, keepdims=True) + 1e-6))
    gc_rows = [jnp.broadcast_to(gcs[u], (C, C)).T for u in U]
    decays = [jnp.where(incl, jnp.exp(jnp.where(incl, gcs[u] - gc_rows[u], 0.0)), 0.0) for u in U]
    kbs = [ks[u] * betas[u] for u in U]
    e_gcs = [jnp.exp(gcs[u]) for u in U]
    n_pows = [-jnp.where(strict, _dot1(kbs[u], ks[u], _NT) * decays[u], 0.0) for u in U]
    invs = [eye + n_pows[u] for u in U]
    for _ in range(int(math.log2(C)) - 1):
        n_pows = [_dot1(n_pows[u], n_pows[u]) for u in U]
        invs = [invs[u] + _dot1(invs[u], n_pows[u]) for u in U]
    sols = [_dot_split_rhs(invs[u], jnp.concatenate([v_all[rs[u], sls[u]] * betas[u], kbs[u] * e_gcs[u]], axis=1))
            for u in U]
    a_qks = [_dot1(qs[u], ks[u], _NT) * decays[u] for u in U]
    for u in U:
        h = units[u][1]
        usol = sols[u][:, :GDN_HEAD_DIM]
        w = sols[u][:, GDN_HEAD_DIM:]
        gc_last = gcs[u][C - 1:C, :]
        k_dec = ks[u] * jnp.exp(gc_last - gcs[u])
        s = s_ref[h]
        v_new = usol - _dot1(w, s)
        o = _dot1(qs[u] * e_gcs[u], s) + _dot1(a_qks[u], v_new)
        s_ref[h] = s * jnp.exp(gc_last) + _dot1(k_dec, v_new, _TN)
        o = o * lax.rsqrt(jnp.mean(o * o, axis=-1, keepdims=True) + EPS) * gn_ref[...]
        o_ref[rs[u], sls[u]] = o * _silu(z_ref[rs[u], sls[u]])


def _gdn(proj, conv_qkv, a_log_row, dt_row, gn_row, batch, seq):
    tt = TILES["gdn_t"]
    assert tt % GDN_CHUNK == 0
    W = GDN_HEADS * GDN_HEAD_DIM
    nt = seq // tt
    m = batch * seq
    wb = W // LANE

    def main(colblk):
        return pl.BlockSpec((tt, W), lambda b, t: (b * nt + t, colblk // wb))

    def halo(colblk):
        return pl.BlockSpec((SUBLANE, W),
                            lambda b, t: (jnp.maximum((b * nt + t) * (tt // SUBLANE) - 1, 0), colblk // wb))

    full = lambda shp: pl.BlockSpec(shp, lambda b, t: (0,) * len(shp))
    return pl.pallas_call(
        _gdn_kernel,
        grid=(batch, nt),
        in_specs=[main(COL_A_Q), main(COL_A_K), main(COL_A_V), main(COL_A_Z),
                  halo(COL_A_Q), halo(COL_A_K), halo(COL_A_V),
                  pl.BlockSpec((tt, LANE), lambda b, t: (b * nt + t, COL_SMALL)),
                  full((GDN_CONV, 3 * W)), full((1, LANE)), full((1, LANE)), full((1, GDN_HEAD_DIM))],
        out_specs=pl.BlockSpec((tt, W), lambda b, t: (b * nt + t, 0)),
        out_shape=jax.ShapeDtypeStruct((m, W), F32),
        scratch_shapes=[pltpu.VMEM((GDN_HEADS, GDN_HEAD_DIM, GDN_HEAD_DIM), F32),
                        pltpu.VMEM((tt + SUBLANE, W), F32)],
        compiler_params=_cparams(("arbitrary", "arbitrary")),
        name="gdn",
    )(proj, proj, proj, proj, proj, proj, proj, proj, conv_qkv, a_log_row, dt_row, gn_row)


def _conf_kernel(lin_ref, gate_ref, z_ref, linh_ref, gateh_ref, cw_ref, cb_ref, g_ref, b_ref,
                 o_ref, cs_ref, sh_ref):
    t = pl.program_id(1)
    tt = lin_ref.shape[0]
    H = CONV_HALO
    has_prev = (t > 0).astype(F32)
    cs_ref[0:H, :] = linh_ref[...] * _sigmoid(gateh_ref[...]) * has_prev
    cs_ref[H:H + tt, :] = lin_ref[...] * _sigmoid(gate_ref[...])
    span = tt + H - SUBLANE
    for r in range(1, SUBLANE):
        sh_ref[r - 1, 0:span, :] = cs_ref[r:r + span, :]
    acc = jnp.zeros(o_ref.shape, F32) + cb_ref[...]
    off = H - (CONV_K - 1)
    for j in range(CONV_K):
        a, r = divmod(off + j, SUBLANE)
        tap = cs_ref[a * SUBLANE:a * SUBLANE + tt, :] if r == 0 else sh_ref[r - 1, a * SUBLANE:a * SUBLANE + tt, :]
        acc = acc + tap * cw_ref[j:j + 1, :]
    mu = jnp.mean(acc, axis=-1, keepdims=True)
    xc = acc - mu
    var = jnp.mean(xc * xc, axis=-1, keepdims=True)
    hb = xc * lax.rsqrt(var + EPS) * g_ref[...] + b_ref[...]
    o_ref[...] = _silu(hb) * _silu(z_ref[...])


def _conf(proj, conv_dw, conv_dw_bias, ln_g, ln_b, batch, seq):
    Wc = BRANCH_WIDTH
    tt = TILES["conf_t"]
    nt = seq // tt
    m = batch * seq
    wb = Wc // LANE
    H = CONV_HALO

    def main(colblk):
        return pl.BlockSpec((tt, Wc), lambda b, t: (b * nt + t, colblk // wb))

    def halo(colblk):
        return pl.BlockSpec((H, Wc), lambda b, t: (jnp.maximum((b * nt + t) * (tt // H) - 1, 0), colblk // wb))

    full = lambda shp: pl.BlockSpec(shp, lambda b, t: (0,) * len(shp))
    return pl.pallas_call(
        _conf_kernel,
        grid=(batch, nt),
        in_specs=[main(COL_B_LIN), main(COL_B_GATE), main(COL_B_Z), halo(COL_B_LIN), halo(COL_B_GATE),
                  full((CONV_K, Wc)), full((1, Wc)), full((1, Wc)), full((1, Wc))],
        out_specs=pl.BlockSpec((tt, Wc), lambda b, t: (b * nt + t, 0)),
        out_shape=jax.ShapeDtypeStruct((m, Wc), F32),
        scratch_shapes=[pltpu.VMEM((tt + H, Wc), F32),
                        pltpu.VMEM((SUBLANE - 1, tt + H - SUBLANE, Wc), F32)],
        compiler_params=_cparams(("arbitrary", "arbitrary")),
        name="conformer_conv",
    )(proj, proj, proj, proj, proj, conv_dw, conv_dw_bias, ln_g, ln_b)


def _merge_head_lanes(accs):
    rows = accs[0].shape[0]
    lane = lax.broadcasted_iota(jnp.int32, (rows, LANE), 1)
    pairs = [jnp.where(lane < ATT_HEAD_DIM, accs[h], accs[h + 1]) for h in range(0, len(accs), HEADS_PER_BLOCK)]
    return jnp.concatenate(pairs, axis=1)


def _sb_block(qh, kt_ref, v_ref, k0, tk, carry_ref, acc_ref, later_ext, r0, mfac):
    d = ATT_HEAD_DIM
    heads = range(len(qh))
    s2 = [jnp.dot(qh[h][r0:], kt_ref[h * d:(h + 1) * d, pl.ds(k0, tk)], preferred_element_type=F32)
          for h in heads]
    lb, lr = [], []
    for h in heads:
        lp = jnp.log(1.0 + jnp.exp2(-jnp.abs(s2[h]))) * LOG2E
        b = jnp.minimum(s2[h], 0.0) - lp
        r = b - s2[h]
        lb.append(b)
        lr.append(r if mfac is None else r * mfac)
    bx = [_dot_sum01(lr[h], later_ext[:, :tk]) for h in heads]
    rsum = [jnp.sum(lr[h], axis=-1, keepdims=True) for h in heads]
    att = []
    for h in heads:
        a = jnp.exp2(lb[h] + bx[h] + carry_ref[h, r0:, :])
        att.append((a if mfac is None else a * mfac).astype(BF16))
    for h in heads:
        blk = h // HEADS_PER_BLOCK
        acc_ref[h, r0:, :] += jnp.dot(att[h], v_ref[pl.ds(k0, tk), blk * LANE:(blk + 1) * LANE],
                                      preferred_element_type=F32)
        carry_ref[h, r0:, :] += rsum[h]


def _sb_kernel(q_ref, kt_ref, v_ref, z_ref, o_ref, carry_ref, acc_ref, *, tq, tk):
    qi = pl.program_id(1)
    d = ATT_HEAD_DIM
    nh = ATT_HEADS
    nsub = tq // tk
    assert tk == LANE and tq % tk == 0
    jr = lax.broadcasted_iota(jnp.int32, (tk, tk + LANE), 0)
    sc = lax.broadcasted_iota(jnp.int32, (tk, tk + LANE), 1)
    later_ext = jnp.logical_or(jr > sc, sc >= tk).astype(BF16)

    q2 = (q_ref[...] * (LOG2E * d ** -0.5)).astype(BF16)
    qh = [q2[:, h * d:(h + 1) * d] for h in range(nh)]
    carry_ref[...] = jnp.zeros_like(carry_ref)
    acc_ref[...] = jnp.zeros_like(acc_ref)
    block = functools.partial(_sb_block, qh, kt_ref, v_ref)

    for c in range(nsub - 1, -1, -1):
        r0 = c * tk
        row = lax.broadcasted_iota(jnp.int32, (tq - r0, tk), 0)
        kidx = lax.broadcasted_iota(jnp.int32, (tq - r0, tk), 1)
        block(pl.multiple_of(qi * tq + r0, tk), tk, carry_ref, acc_ref, later_ext, r0, (kidx < row).astype(F32))

    nold = qi * nsub

    def live():
        mx = carry_ref[0]
        for h in range(1, nh):
            mx = jnp.maximum(mx, carry_ref[h])
        return (jnp.max(mx) > EXP2_ZERO_BELOW).astype(jnp.int32)

    def cond(st):
        return jnp.logical_and(st[0] < nold, st[1] > 0)

    def body(st):
        block(pl.multiple_of((nold - 1 - st[0]) * tk, tk), tk, carry_ref, acc_ref, later_ext, 0, None)
        return st[0] + 1, live()

    lax.while_loop(cond, body, (jnp.int32(0), live()))
    o_ref[...] = _merge_head_lanes([acc_ref[h] for h in range(nh)]) * _silu(z_ref[...])


def _sb_attention(proj, kt_all, v_all, batch, seq):
    tq, tk = TILES["att_q"], TILES["sb_k"]
    m = batch * seq
    nq = seq // tq
    W = ATT_HEADS * ATT_HEAD_DIM
    wb = W // LANE
    once = pl.Buffered(1)
    return pl.pallas_call(
        functools.partial(_sb_kernel, tq=tq, tk=tk),
        grid=(batch, nq),
        in_specs=[pl.BlockSpec((tq, W), lambda b, i: (b * nq + i, COL_C_Q // wb)),
                  pl.BlockSpec((None, W, seq), lambda b, i: (b, 0, 0), pipeline_mode=once),
                  pl.BlockSpec((seq, W), lambda b, i: (b, COL_V_C // wb), pipeline_mode=once),
                  pl.BlockSpec((tq, W), lambda b, i: (b * nq + i, COL_C_Z // wb))],
        out_specs=pl.BlockSpec((tq, W), lambda b, i: (b * nq + i, 0)),
        out_shape=jax.ShapeDtypeStruct((m, W), F32),
        scratch_shapes=[pltpu.VMEM((ATT_HEADS, tq, LANE), F32), pltpu.VMEM((ATT_HEADS, tq, LANE), F32)],
        compiler_params=_cparams(("arbitrary", "arbitrary")),
        name="stick_breaking_attention",
    )(proj, kt_all, v_all, proj)


def _fcum_kernel(sm_ref, fb_ref, ft_ref, pm_ref, carry_ref, pmin_ref, *, blk):
    t = pl.program_id(1)
    tt = sm_ref.shape[0]

    @pl.when(t == 0)
    def _():
        carry_ref[...] = jnp.zeros_like(carry_ref)
        pmin_ref[...] = jnp.zeros_like(pmin_ref)

    x = sm_ref[...] + fb_ref[...]
    log_f = jnp.minimum(x, 0.0) - jnp.log1p(jnp.exp(-jnp.abs(x)))
    row = lax.broadcasted_iota(jnp.int32, (tt, tt), 0)
    col = lax.broadcasted_iota(jnp.int32, (tt, tt), 1)
    f = _dot_exact_lhs((row >= col).astype(F32), log_f) + carry_ref[...]
    carry_ref[...] = f[tt - 1:tt, :]
    ft_ref[...] = f.T[SMALL_F:SMALL_F + ATT_HEADS, :]
    rid = lax.broadcasted_iota(jnp.int32, (SUBLANE, LANE), 0)
    pm = pmin_ref[...]
    out = jnp.zeros((SUBLANE, LANE), F32)
    for r in range(tt // blk):
        pm = jnp.minimum(pm, jnp.min(f[r * blk:(r + 1) * blk], axis=0, keepdims=True))
        out = jnp.where(rid == r, pm, out)
    pmin_ref[...] = pm
    pm_ref[...] = out


def _fcum(proj, fb_row, batch, seq):
    tt, blk = TILES["fcum_t"], TILES["fox_k"]
    nt = seq // tt
    per = tt // blk
    ft, pm = pl.pallas_call(
        functools.partial(_fcum_kernel, blk=blk),
        grid=(batch, nt),
        in_specs=[pl.BlockSpec((tt, LANE), lambda b, t: (b * nt + t, COL_SMALL)),
                  pl.BlockSpec((1, LANE), lambda b, t: (0, 0))],
        out_specs=[pl.BlockSpec((None, ATT_HEADS, tt), lambda b, t: (b, 0, t)),
                   pl.BlockSpec((None, SUBLANE, LANE), lambda b, t: (b * nt + t, 0, 0))],
        out_shape=[jax.ShapeDtypeStruct((batch, ATT_HEADS, seq), F32),
                   jax.ShapeDtypeStruct((batch * nt, SUBLANE, LANE), F32)],
        scratch_shapes=[pltpu.VMEM((1, LANE), F32), pltpu.VMEM((1, LANE), F32)],
        compiler_params=_cparams(("arbitrary", "arbitrary")),
        name="forget_cumsum",
    )(proj, fb_row)
    pm = pm[:, :per, SMALL_F:SMALL_F + ATT_HEADS].reshape(batch, nt * per, ATT_HEADS)
    return ft, jnp.transpose(pm, (0, 2, 1)).reshape(-1)


def _fox_block(qh, kt_ref, v_ref, fk_ref, cref, k0, tk, m_ref, acc_ref, mode, heads):
    d = ATT_HEAD_DIM
    pv = lambda pr, h: jnp.dot(pr, v_ref[pl.ds(k0, tk), h * LANE:(h + 1) * LANE], preferred_element_type=F32)
    z = {}
    for h in heads:
        fk2 = (fk_ref[h:h + 1, pl.ds(k0, tk)] - cref[h]) * LOG2E
        z[h] = jnp.dot(qh[h], kt_ref[h * d:(h + 1) * d, pl.ds(k0, tk)], preferred_element_type=F32) - fk2
    if mode == "plain":
        pr = {h: jnp.exp2(z[h] - m_ref[h]).astype(BF16) for h in heads}
        for h in heads:
            acc_ref[h] += pv(pr[h], h)
        return
    if mode == "first":
        shp = (qh[0].shape[0], tk)
        keep = lax.broadcasted_iota(jnp.int32, shp, 1) <= lax.broadcasted_iota(jnp.int32, shp, 0)
        z = {h: jnp.where(keep, z[h], NEG_BIG) for h in heads}
        m_new = {h: jnp.max(z[h], axis=-1, keepdims=True) for h in heads}
    else:
        m_old = {h: m_ref[h] for h in heads}
        m_new = {h: jnp.maximum(m_old[h], jnp.max(z[h], axis=-1, keepdims=True)) for h in heads}
    pr = {h: jnp.exp2(z[h] - m_new[h]).astype(BF16) for h in heads}
    for h in heads:
        if mode == "first":
            acc_ref[h] = pv(pr[h], h)
        else:
            acc_ref[h] = jnp.exp2(m_old[h] - m_new[h]) * acc_ref[h] + pv(pr[h], h)
        m_ref[h] = m_new[h]


def _fox_kernel(pm_ref, q_ref, kt_ref, v_ref, z_ref, fk_ref, o_ref, kmax_ref, m_ref, acc_ref, *, tq, tk, seq):
    b = pl.program_id(0)
    qi = pl.program_id(1)
    d = ATT_HEAD_DIM
    nh = ATT_HEADS
    nblk = seq // tk
    assert tq == tk

    @pl.when(qi == 0)
    def _():
        step = 2048 if seq % 2048 == 0 else tk
        for h in range(nh):
            best = jnp.zeros((1, step), F32)
            for c0 in range(0, seq, step):
                kf = kt_ref[h * d:(h + 1) * d, c0:c0 + step].astype(F32)
                best = jnp.maximum(best, jnp.sum(kf * kf, axis=0, keepdims=True))
            kmax_ref[h] = jnp.broadcast_to(jnp.sqrt(jnp.max(best, axis=-1, keepdims=True)), (SUBLANE, LANE))

    q2 = (q_ref[...] * (LOG2E * d ** -0.5)).astype(BF16)

    qh, ub, base, cref = [], [], [], []
    for h in range(nh):
        qh.append(q2[:, h * d:(h + 1) * d])
        qf = qh[h].astype(F32)
        qn = jnp.sqrt(jnp.sum(qf * qf, axis=-1, keepdims=True))
        ub.append(qn * kmax_ref[h, 0:1, 0:1] * NORM_SLACK)
        base.append((b * nh + h) * nblk)
        cref.append(jnp.where(qi > 0, pm_ref[base[h] + jnp.maximum(qi - 1, 0)], 0.0))

    block = functools.partial(_fox_block, qh, kt_ref, v_ref, fk_ref, cref)
    all_heads = list(range(nh))
    block(pl.multiple_of(qi * tk, tk), tk, m_ref, acc_ref, "first", all_heads)

    def slack(heads, old=None):
        return [jnp.max(ub[h] - m_ref[h]) if h in heads else old[h] for h in range(nh)]

    def excess(sl, j, heads):
        jj = jnp.maximum(j, 0)
        w = None
        for h in heads:
            t = sl[h] + (cref[h] - pm_ref[base[h] + jj]) * LOG2E
            w = t if w is None else jnp.maximum(w, t)
        return w

    def sweep(st, heads, watch):
        def cond(st):
            return jnp.logical_and(st[0] < qi, excess(st[1:], qi - 1 - st[0], watch) > EXP2_ZERO_BELOW)

        def body(st):
            i = st[0]
            j = qi - 1 - i
            k0 = pl.multiple_of(j * tk, tk)
            sl = list(st[1:])

            def plain():
                block(k0, tk, m_ref, acc_ref, "plain", heads)
                return tuple(sl)

            def online():
                block(k0, tk, m_ref, acc_ref, "online", heads)
                return tuple(slack(heads, sl))

            return (i + 1, *lax.cond(excess(sl, j, heads) <= 0.0, plain, online))

        return lax.while_loop(cond, body, st)

    st = (jnp.int32(0), *slack(all_heads))
    heads = all_heads
    while heads:
        half = max(len(heads) // 2, 1) if len(heads) > HEADS_PER_BLOCK else len(heads)
        st = sweep(st, heads, heads[:half])
        heads = heads[half:]
    outs = [acc_ref[h] / pltpu.roll(acc_ref[h], d, 1) for h in range(nh)]
    o_ref[...] = _merge_head_lanes(outs) * _silu(z_ref[...])


def _fox_attention(proj, kt_all, v_all, f_t, pm_tab, batch, seq):
    tq, tk = TILES["att_q"], TILES["fox_k"]
    m = batch * seq
    nq = seq // tq
    W = ATT_HEADS * ATT_HEAD_DIM
    wb = W // LANE
    assert COL_V_D == 0
    once = pl.Buffered(1)
    return pl.pallas_call(
        functools.partial(_fox_kernel, tq=tq, tk=tk, seq=seq),
        grid=(batch, nq),
        in_specs=[pl.BlockSpec(memory_space=pltpu.SMEM),
                  pl.BlockSpec((tq, W), lambda b, i: (b * nq + i, COL_D_Q // wb)),
                  pl.BlockSpec((None, W, seq), lambda b, i: (b, 1, 0), pipeline_mode=once),
                  pl.BlockSpec((seq, ATT_HEADS * LANE), lambda b, i: (b, 0), pipeline_mode=once),
                  pl.BlockSpec((tq, W), lambda b, i: (b * nq + i, COL_D_Z // wb)),
                  pl.BlockSpec((None, ATT_HEADS, seq), lambda b, i: (b, 0, 0), pipeline_mode=once)],
        out_specs=pl.BlockSpec((tq, W), lambda b, i: (b * nq + i, 0)),
        out_shape=jax.ShapeDtypeStruct((m, W), F32),
        scratch_shapes=[pltpu.VMEM((ATT_HEADS, SUBLANE, LANE), F32),
                        pltpu.VMEM((ATT_HEADS, tq, 1), F32),
                        pltpu.VMEM((ATT_HEADS, tq, LANE), F32)],
        compiler_params=_cparams(("arbitrary", "arbitrary")),
        name="forgetting_attention",
    )(pm_tab, proj, kt_all, v_all, proj, f_t)


def _epi_kernel(x_ref, ya_ref, yb_ref, yc_ref, yd_ref, p_ref, wg_ref, bg_ref, wb_ref, wo_ref,
                wp_ref, wpg_ref, bpg_ref, lg_ref, lb_ref, o_ref, *, alpha):
    dm = x_ref.shape[1]
    x16 = x_ref[...].astype(BF16)
    merged = jnp.zeros(x_ref.shape, F32)
    for br, y_ref in enumerate((ya_ref, yb_ref, yc_ref, yd_ref)):
        cs = slice(br * dm, (br + 1) * dm)
        gate = _sigmoid(jnp.dot(x16, wg_ref[:, cs], preferred_element_type=F32) + bg_ref[:, cs])
        merged = merged + gate * jnp.dot(y_ref[...].astype(BF16), wb_ref[br], preferred_element_type=F32)
    mix = jnp.dot(merged.astype(BF16), wo_ref[...], preferred_element_type=F32)
    r = alpha * x_ref[...] + mix
    pg = _sigmoid(jnp.dot(r.astype(BF16), wpg_ref[...], preferred_element_type=F32) + bpg_ref[...])
    r = r + pg * jnp.dot(p_ref[...].astype(BF16), wp_ref[...], preferred_element_type=F32)
    mu = jnp.mean(r, axis=-1, keepdims=True)
    rc = r - mu
    var = jnp.mean(rc * rc, axis=-1, keepdims=True)
    o_ref[...] = rc * lax.rsqrt(var + EPS) * lg_ref[...] + lb_ref[...]


def _epilogue(x2d, ya, yb, yc, yd, p2d, w_gates, b_gate, w_branch, w_out, w_ple, w_ple_gate, b_ple_gate,
              ln_g, ln_b, alpha):
    m, dm = x2d.shape
    tm = TILES["epi_m"]
    row = lambda w: pl.BlockSpec((tm, w), lambda i: (i, 0))
    once = pl.Buffered(1)
    full = lambda shp: pl.BlockSpec(shp, lambda i: (0,) * len(shp), pipeline_mode=once)
    return pl.pallas_call(
        functools.partial(_epi_kernel, alpha=alpha),
        grid=(m // tm,),
        in_specs=[row(dm),
                  row(BRANCH_WIDTH), row(BRANCH_WIDTH), row(BRANCH_WIDTH), row(BRANCH_WIDTH),
                  row(p2d.shape[1]),
                  full(w_gates.shape), full((1, N_BRANCH * dm)), full(w_branch.shape), full(w_out.shape),
                  full(w_ple.shape), full(w_ple_gate.shape), full((1, dm)), full((1, dm)), full((1, dm))],
        out_specs=row(dm),
        out_shape=jax.ShapeDtypeStruct((m, dm), F32),
        compiler_params=_cparams(("arbitrary",)),
        name="epilogue",
    )(x2d, ya, yb, yc, yd, p2d, w_gates, b_gate, w_branch, w_out, w_ple, w_ple_gate, b_ple_gate, ln_g, ln_b)


def _split_w_in(w_in, order_d):
    gw, cw, aw = GDN_HEADS * GDN_HEAD_DIM, BRANCH_WIDTH, ATT_HEADS * ATT_HEAD_DIM
    d = ATT_HEAD_DIM
    dm = w_in.shape[0]
    sizes = ([gw] * 4 + [GDN_HEADS, GDN_HEADS] + [2 * cw, cw] + [aw] * 4 + [aw] * 4 + [ATT_HEADS]
             + [N_BRANCH * dm])
    assert sum(sizes) == w_in.shape[1]
    pts = [sum(sizes[:i + 1]) for i in range(len(sizes) - 1)]
    (qa, ka, va, za, aa, ba, glu, zb, qc, kc, vc, zc, qd, kd, vd, zd, fd, gates) = jnp.split(w_in, pts, axis=1)
    by_head = lambda w: w.reshape(dm, ATT_HEADS, d)[:, order_d, :].reshape(dm, aw)
    qd, kd, vd, zd, fd = by_head(qd), by_head(kd), by_head(vd), by_head(zd), fd[:, order_d]
    small = jnp.concatenate([aa, ba, fd], axis=1)
    small = jnp.pad(small, ((0, 0), (0, LANE - small.shape[1])))
    w_main = jnp.concatenate([qa, ka, va, za, glu, zb, qc, zc, qd, zd, small], axis=1)
    assert w_main.shape[1] == PROJ_COLS
    w_kt = jnp.concatenate([kc, kd], axis=1).T
    zeros = jnp.zeros((dm, d), w_in.dtype)
    vd_blocks, bias_blocks = [], []
    for h in range(ATT_HEADS):
        vh = vd[:, h * d:(h + 1) * d]
        pair = [vh, zeros] if h % 2 == 0 else [zeros, vh]
        ones = [jnp.zeros((d,), F32), jnp.ones((d,), F32)]
        vd_blocks += pair
        bias_blocks += ones if h % 2 == 0 else ones[::-1]
    w_v = jnp.concatenate(vd_blocks + [vc], axis=1)
    b_v = jnp.concatenate(bias_blocks + [jnp.zeros((aw,), F32)]).reshape(1, -1)
    assert w_v.shape[1] == V_COLS
    return w_main, w_kt, w_v, b_v, gates


def _lane_row(vec, offset):
    return jnp.zeros((1, LANE), F32).at[0, offset:offset + vec.shape[0]].set(vec.astype(F32))


def _layer(x2d, p2d, batch, seq, alpha, w_in, b_gate, conv_qkv, a_log, dt_bias, gdn_norm, conv_dw,
           conv_dw_bias, conv_ln_g, conv_ln_b, forget_bias, w_branch, w_out, w_ple, w_ple_gate,
           b_ple_gate, ln_g, ln_b):
    order_d = jnp.argsort(forget_bias)
    forget_bias = forget_bias[order_d]
    wd = w_branch[3].reshape(ATT_HEADS, ATT_HEAD_DIM, -1)[order_d].reshape(w_branch.shape[1:])
    w_branch = jnp.concatenate([w_branch[:3], wd[None]], axis=0)
    w_main, w_kt, w_v, b_v, w_gates = _split_w_in(w_in.astype(BF16), order_d)
    proj, v_all, kt_all = _proj(x2d, w_main, w_v, b_v, w_kt, batch, seq)

    ya = _gdn(proj, conv_qkv, _lane_row(a_log, SMALL_A), _lane_row(dt_bias, SMALL_A),
              gdn_norm.reshape(1, -1), batch, seq)
    yb = _conf(proj, conv_dw, conv_dw_bias.reshape(1, -1), conv_ln_g.reshape(1, -1),
               conv_ln_b.reshape(1, -1), batch, seq)
    yc = _sb_attention(proj, kt_all, v_all, batch, seq)
    f_t, pm_tab = _fcum(proj, _lane_row(forget_bias, SMALL_F), batch, seq)
    yd = _fox_attention(proj, kt_all, v_all, f_t, pm_tab, batch, seq)
    return _epilogue(x2d, ya, yb, yc, yd, p2d, w_gates, b_gate.reshape(1, -1), w_branch.astype(BF16),
                     w_out.astype(BF16), w_ple.astype(BF16), w_ple_gate.astype(BF16),
                     b_ple_gate.reshape(1, -1), ln_g.reshape(1, -1), ln_b.reshape(1, -1), alpha)


def kernel(x, p, w_in, b_gate, conv_qkv, a_log, dt_bias, gdn_norm, conv_dw, conv_dw_bias, conv_ln_g,
           conv_ln_b, forget_bias, w_branch, w_out, w_ple, w_ple_gate, b_ple_gate, ln_g, ln_b):
    batch, seq, dm = x.shape
    depth = w_in.shape[0]
    alpha = (2 * depth) ** 0.25
    x2d = x.reshape(batch * seq, dm)
    for i in range(depth):
        x2d = _layer(x2d, p[i].reshape(batch * seq, -1), batch, seq, alpha, w_in[i], b_gate[i],
                     conv_qkv[i], a_log[i], dt_bias[i], gdn_norm[i], conv_dw[i], conv_dw_bias[i],
                     conv_ln_g[i], conv_ln_b[i], forget_bias[i], w_branch[i], w_out[i], w_ple[i],
                     w_ple_gate[i], b_ple_gate[i], ln_g[i], ln_b[i])
    return x2d.reshape(batch, seq, dm)
```
